```python
import jax, jax.numpy as jnp
from jax import lax
import numpy as np

D_MODEL = 2048
BATCH = 4
SEQ = 4096
DEPTH = 4

LRU_WIDTH = D_MODEL // 2
LRU_BLOCKS = 8
LRU_BLOCK_W = LRU_WIDTH // LRU_BLOCKS
CONV_WIDTH = 4
LRU_C = 8.0
HEAD_DIM = 128
ATTN_HEADS = (D_MODEL // 2) // HEAD_DIM
ATTN_WIDTH = ATTN_HEADS * HEAD_DIM
MOBA_BLOCK = 256
MOBA_TOPK = 3
Q_BLOCK = 128
ROPE_THETA = 500000.0
ROT_DIM = HEAD_DIM // 4
NEG_INF = -1e30
D_FF = 5632
N_EXPERTS = 8
TOP_K = 2
N_DENSE = (DEPTH + 1) // 2
N_MOE = DEPTH // 2
IN_COLS = 2 * LRU_WIDTH + 3 * ATTN_WIDTH + 2 * D_MODEL
SPLITS = (LRU_WIDTH, 2 * LRU_WIDTH, 2 * LRU_WIDTH + ATTN_WIDTH, 2 * LRU_WIDTH + 2 * ATTN_WIDTH,
          2 * LRU_WIDTH + 3 * ATTN_WIDTH, 2 * LRU_WIDTH + 3 * ATTN_WIDTH + D_MODEL)
DEEPNORM_ALPHA = (2.0 * DEPTH) ** 0.25
DEEPNORM_BETA = (8.0 * DEPTH) ** -0.25
LN_EPS = 1e-5

kernel_name = 'hybrid_rglru_moba_moe_deepnorm_adaln'


def layer_norm(x, g, b):
    xf = x.astype(jnp.float32)
    mu = xf.mean(-1, keepdims=True)
    var = jnp.square(xf - mu).mean(-1, keepdims=True)
    return ((xf - mu) * lax.rsqrt(var + LN_EPS) * g.astype(jnp.float32) + b.astype(jnp.float32)).astype(x.dtype)


def causal_depthwise_conv(x, w, b):
    y = lax.conv_general_dilated(x, w[:, None, :].astype(x.dtype), window_strides=(1,),
                                 padding=[(CONV_WIDTH - 1, 0)],
                                 dimension_numbers=('NWC', 'WIO', 'NWC'),
                                 feature_group_count=x.shape[-1])
    return y + b


def block_diag_linear(x, w, b):
    xb = x.reshape(x.shape[:-1] + (LRU_BLOCKS, LRU_BLOCK_W))
    y = jnp.einsum('bsni,nio->bsno', xb, w)
    return y.reshape(x.shape) + b


def rg_lru(x, w_r, b_r, w_i, b_i, lam):
    xf = x.astype(jnp.float32)
    r = jax.nn.sigmoid(block_diag_linear(x, w_r, b_r).astype(jnp.float32))
    i = jax.nn.sigmoid(block_diag_linear(x, w_i, b_i).astype(jnp.float32))
    log_a = -LRU_C * r * jax.nn.softplus(-lam.astype(jnp.float32))
    a = jnp.exp(log_a)
    u = jnp.sqrt(-jnp.expm1(2.0 * log_a)) * (i * xf)

    def combine(left, right):
        a_l, u_l = left
        a_r, u_r = right
        return a_l * a_r, a_r * u_l + u_r

    _, h = lax.associative_scan(combine, (a, u), axis=1)
    return h.astype(x.dtype)


def partial_rotary(x, cos, sin):
    half = ROT_DIM // 2
    x1 = x[..., :half]
    x2 = x[..., half:ROT_DIM]
    rot = jnp.concatenate([x1 * cos - x2 * sin, x2 * cos + x1 * sin], axis=-1)
    return jnp.concatenate([rot.astype(x.dtype), x[..., ROT_DIM:]], axis=-1)


def moba_attention(q, k, v):
    B, H, S, Dh = q.shape
    n_blk = -(-S // MOBA_BLOCK)
    pad = n_blk * MOBA_BLOCK - S
    k_p = jnp.pad(k, ((0, 0), (0, 0), (0, pad), (0, 0)))
    v_p = jnp.pad(v, ((0, 0), (0, 0), (0, pad), (0, 0)))
    kb = k_p.reshape(B, H, n_blk, MOBA_BLOCK, Dh)
    vb = v_p.reshape(B, H, n_blk, MOBA_BLOCK, Dh)
    k_mean = kb.astype(jnp.float32).mean(axis=3)
    top_k = min(MOBA_TOPK, n_blk)
    scale = HEAD_DIM ** -0.5
    b_idx = jnp.arange(B)[:, None, None, None]
    h_idx = jnp.arange(H)[None, :, None, None]
    blk_ids = jnp.arange(n_blk)

    def one_query_block(qs):
        qc = lax.dynamic_slice_in_dim(q, qs, Q_BLOCK, axis=2)
        own = qs // MOBA_BLOCK
        qpos = qs + jnp.arange(Q_BLOCK)
        gate = jnp.einsum('bhqd,bhnd->bhqn', qc.astype(jnp.float32), k_mean)
        gate = jnp.where(blk_ids < own, gate, -jnp.inf)
        _, sel = lax.top_k(gate, top_k)
        sel_valid = sel < own
        kg = kb[b_idx, h_idx, sel]
        vg = vb[b_idx, h_idx, sel]
        s_sel = jnp.einsum('bhqd,bhqknd->bhqkn', qc, kg).astype(jnp.float32) * scale
        s_sel = jnp.where(sel_valid[..., None], s_sel, NEG_INF)
        k_own = lax.dynamic_slice_in_dim(k_p, own * MOBA_BLOCK, MOBA_BLOCK, axis=2)
        v_own = lax.dynamic_slice_in_dim(v_p, own * MOBA_BLOCK, MOBA_BLOCK, axis=2)
        s_own = jnp.einsum('bhqd,bhnd->bhqn', qc, k_own).astype(jnp.float32) * scale
        kpos = own * MOBA_BLOCK + jnp.arange(MOBA_BLOCK)
        s_own = jnp.where(kpos[None, :] <= qpos[:, None], s_own, NEG_INF)
        s = jnp.concatenate([s_own, s_sel.reshape(B, H, Q_BLOCK, top_k * MOBA_BLOCK)], axis=-1)
        p = jax.nn.softmax(s, axis=-1).astype(v.dtype)
        p_own = p[..., :MOBA_BLOCK]
        p_sel = p[..., MOBA_BLOCK:].reshape(B, H, Q_BLOCK, top_k, MOBA_BLOCK)
        return (jnp.einsum('bhqn,bhnd->bhqd', p_own, v_own)
                + jnp.einsum('bhqkn,bhqknd->bhqd', p_sel, vg))

    starts = jnp.arange(S // Q_BLOCK, dtype=jnp.int32) * Q_BLOCK
    out = lax.map(one_query_block, starts)
    return out.transpose(1, 0, 3, 2, 4).reshape(B, S, H * Dh)


def hybrid_mixer(h, w_in, conv_w, conv_b, w_r, b_r, w_i, b_i, lam, w_pa, w_pb, w_out, cos, sin):
    B, S, _ = h.shape
    proj = h @ w_in
    x_lru, g_lru, q, k, v, g_a, g_b = jnp.split(proj, SPLITS, axis=-1)
    xa = causal_depthwise_conv(x_lru, conv_w, conv_b)
    y_a = (rg_lru(xa, w_r, b_r, w_i, b_i, lam) * jax.nn.gelu(g_lru)) @ w_pa
    def heads(t):
        return t.reshape(B, S, ATTN_HEADS, HEAD_DIM).transpose(0, 2, 1, 3)
    qh = partial_rotary(heads(q), cos, sin)
    kh = partial_rotary(heads(k), cos, sin)
    y_b = moba_attention(qh, kh, heads(v)) @ w_pb
    merged = jax.nn.sigmoid(g_a) * y_a + jax.nn.sigmoid(g_b) * y_b
    return merged @ w_out


def swiglu(h, w_gate, w_up, w_down):
    return (jax.nn.silu(h @ w_gate) * (h @ w_up)) @ w_down


def moe_swiglu(h, w_router, b_router, w_gate, w_up, w_down):
    B, S, D = h.shape
    t = h.reshape(B * S, D)
    logits = (t @ w_router + b_router).astype(jnp.float32)
    top_logit, top_idx = lax.top_k(logits, TOP_K)
    top_w = jax.nn.softmax(top_logit, axis=-1)
    combine = jnp.sum(jax.nn.one_hot(top_idx, N_EXPERTS, dtype=jnp.float32) * top_w[..., None], axis=-2)
    combine = combine.astype(t.dtype)
    out = jnp.zeros_like(t)
    for e in range(N_EXPERTS):
        out = out + combine[:, e:e + 1] * swiglu(t, w_gate[e], w_up[e], w_down[e])
    return out.reshape(B, S, D)


def setup_inputs(seed: int = 0) -> dict:
    key = jax.random.key(seed)
    ks = jax.random.split(key, 32)
    f32 = jnp.float32

    def nrm(k, shape, fan_in, gain=1.0):
        return jax.random.normal(k, shape, f32) * (gain * fan_in ** -0.5)

    def small(k, shape, s=0.02):
        return jax.random.normal(k, shape, f32) * s

    a_pow_c = jax.random.uniform(ks[9], (DEPTH, LRU_WIDTH), f32, minval=0.9, maxval=0.999)
    a_base = a_pow_c ** (1.0 / LRU_C)
    lru_lambda = jnp.log(a_base) - jnp.log1p(-a_base)
    return {
        'x': jax.random.normal(ks[0], (BATCH, SEQ, D_MODEL), f32),
        'c': jax.random.normal(ks[1], (BATCH, D_MODEL), f32),
        'w_in': nrm(ks[2], (DEPTH, D_MODEL, IN_COLS), D_MODEL),
        'conv_w': nrm(ks[3], (DEPTH, CONV_WIDTH, LRU_WIDTH), CONV_WIDTH),
        'conv_b': small(ks[4], (DEPTH, LRU_WIDTH)),
        'w_rgate': nrm(ks[5], (DEPTH, LRU_BLOCKS, LRU_BLOCK_W, LRU_BLOCK_W), LRU_BLOCK_W),
        'b_rgate': small(ks[6], (DEPTH, LRU_WIDTH)),
        'w_igate': nrm(ks[7], (DEPTH, LRU_BLOCKS, LRU_BLOCK_W, LRU_BLOCK_W), LRU_BLOCK_W),
        'b_igate': small(ks[8], (DEPTH, LRU_WIDTH)),
        'lru_lambda': lru_lambda,
        'w_proj_a': nrm(ks[10], (DEPTH, LRU_WIDTH, D_MODEL), LRU_WIDTH),
        'w_proj_b': nrm(ks[11], (DEPTH, ATTN_WIDTH, D_MODEL), ATTN_WIDTH),
        'w_out': nrm(ks[12], (DEPTH, D_MODEL, D_MODEL), D_MODEL, DEEPNORM_BETA),
        'w_ada': nrm(ks[13], (DEPTH, D_MODEL, 6 * D_MODEL), D_MODEL, 0.1),
        'b_ada': small(ks[14], (DEPTH, 6 * D_MODEL)),
        'ln_mix_g': 1.0 + small(ks[15], (DEPTH, D_MODEL)),
        'ln_mix_b': small(ks[16], (DEPTH, D_MODEL)),
        'ln_ffn_g': 1.0 + small(ks[17], (DEPTH, D_MODEL)),
        'ln_ffn_b': small(ks[18], (DEPTH, D_MODEL)),
        'ffn_w_gate': nrm(ks[19], (N_DENSE, D_MODEL, D_FF), D_MODEL),
        'ffn_w_up': nrm(ks[20], (N_DENSE, D_MODEL, D_FF), D_MODEL),
        'ffn_w_down': nrm(ks[21], (N_DENSE, D_FF, D_MODEL), D_FF, DEEPNORM_BETA),
        'moe_w_router': nrm(ks[22], (N_MOE, D_MODEL, N_EXPERTS), D_MODEL),
        'moe_b_router': small(ks[23], (N_MOE, N_EXPERTS), 0.01),
        'moe_w_gate': nrm(ks[24], (N_MOE, N_EXPERTS, D_MODEL, D_FF), D_MODEL),
        'moe_w_up': nrm(ks[25], (N_MOE, N_EXPERTS, D_MODEL, D_FF), D_MODEL),
        'moe_w_down': nrm(ks[26], (N_MOE, N_EXPERTS, D_FF, D_MODEL), D_FF, DEEPNORM_BETA),
    }


def reference(x, c, w_in, conv_w, conv_b, w_rgate, b_rgate, w_igate, b_igate, lru_lambda,
              w_proj_a, w_proj_b, w_out, w_ada, b_ada, ln_mix_g, ln_mix_b, ln_ffn_g, ln_ffn_b,
              ffn_w_gate, ffn_w_up, ffn_w_down, moe_w_router, moe_b_router, moe_w_gate,
              moe_w_up, moe_w_down):
    S = x.shape[1]
    pos = jnp.arange(S, dtype=jnp.float32)
    inv_freq = ROPE_THETA ** (-jnp.arange(0, ROT_DIM, 2, dtype=jnp.float32) / ROT_DIM)
    ang = pos[:, None] * inv_freq[None, :]
    cos, sin = jnp.cos(ang), jnp.sin(ang)
    c_act = jax.nn.silu(c)
    for l in range(DEPTH):
        ada = c_act @ w_ada[l] + b_ada[l]
        sh1, sc1, g1, sh2, sc2, g2 = jnp.split(ada[:, None, :], 6, axis=-1)
        h = x * (1.0 + sc1) + sh1
        y = hybrid_mixer(h, w_in[l], conv_w[l], conv_b[l], w_rgate[l], b_rgate[l], w_igate[l],
                         b_igate[l], lru_lambda[l], w_proj_a[l], w_proj_b[l], w_out[l], cos, sin)
        x = layer_norm(DEEPNORM_ALPHA * x + (1.0 + g1) * y, ln_mix_g[l], ln_mix_b[l])
        h = x * (1.0 + sc2) + sh2
        if l % 2 == 0:
            j = l // 2
            y = swiglu(h, ffn_w_gate[j], ffn_w_up[j], ffn_w_down[j])
        else:
            j = l // 2
            y = moe_swiglu(h, moe_w_router[j], moe_b_router[j], moe_w_gate[j], moe_w_up[j], moe_w_down[j])
        x = layer_norm(DEEPNORM_ALPHA * x + (1.0 + g2) * y, ln_ffn_g[l], ln_ffn_b[l])
    return x
```

```python
import functools

import jax
import jax.numpy as jnp
from jax import lax
from jax.experimental import pallas as pl
from jax.experimental.pallas import tpu as pltpu

F32 = jnp.float32
BF16 = jnp.bfloat16

LRU_BLOCKS = 8
LRU_BLOCK_W = 128
CONV_WIDTH = 4
LRU_C = 8.0
HEAD_DIM = 128
MOBA_BLOCK = 256
MOBA_TOPK = 3
ROPE_THETA = 500000.0
ROT_DIM = HEAD_DIM // 4
ROT_HALF = ROT_DIM // 2
NEG_INF = -1e30
N_EXPERTS = 8
LN_EPS = 1e-5
LANES = 128
SUBLANES = 8
MIB = 1024 * 1024


def _cparams(semantics, vmem_mib):
    return pltpu.CompilerParams(dimension_semantics=semantics, vmem_limit_bytes=vmem_mib * MIB)


def _resident(block_shape, index_map):
    return pl.BlockSpec(block_shape, index_map, pipeline_mode=pl.Buffered(1))


def _ada_kernel(c_ref, w_ref, b_ref, o_ref):
    c = c_ref[...]
    c_act = c * jax.nn.sigmoid(c)
    o_ref[0] = jnp.dot(c_act, w_ref[0], preferred_element_type=F32) + b_ref[0]


def _ada_all_layers(c_pad, w_ada, b_ada, tn=1024):
    depth, d, n = w_ada.shape
    rows = c_pad.shape[0]
    return pl.pallas_call(
        _ada_kernel,
        grid=(depth, n // tn),
        in_specs=[
            pl.BlockSpec((rows, d), lambda l, j: (0, 0)),
            pl.BlockSpec((1, d, tn), lambda l, j: (l, 0, j)),
            pl.BlockSpec((1, 1, tn), lambda l, j: (l, 0, j)),
        ],
        out_specs=pl.BlockSpec((1, rows, tn), lambda l, j: (l, 0, j)),
        out_shape=jax.ShapeDtypeStruct((depth, rows, n), F32),
        compiler_params=_cparams(("arbitrary", "arbitrary"), 40),
        name="ada",
    )(c_pad, w_ada, b_ada.reshape(depth, 1, n))


def _modulate_kernel(x_ref, ep_ref, h_ref):
    ep = ep_ref[0]
    h_ref[...] = (x_ref[...] * (1.0 + ep[1:2]) + ep[2:3]).astype(h_ref.dtype)


def _modulate(x2d, ep, seq, tm=512):
    t, d = x2d.shape
    per_b = seq // tm
    return pl.pallas_call(
        _modulate_kernel,
        grid=(t // tm,),
        in_specs=[
            pl.BlockSpec((tm, d), lambda i: (i, 0)),
            pl.BlockSpec((1, 3, d), lambda i: (i // per_b, 0, 0)),
        ],
        out_specs=pl.BlockSpec((tm, d), lambda i: (i, 0)),
        out_shape=jax.ShapeDtypeStruct((t, d), BF16),
        compiler_params=_cparams(("arbitrary",), 32),
        name="modulate",
    )(x2d, ep)


def _matmul_kernel(a_ref, w_ref, o_ref):
    o_ref[...] = jnp.dot(a_ref[...], w_ref[...], preferred_element_type=F32).astype(o_ref.dtype)


def _matmul(a, w, tm=1024, tn=1024):
    m, k = a.shape
    n = w.shape[1]
    return pl.pallas_call(
        _matmul_kernel,
        grid=(n // tn, m // tm),
        in_specs=[
            pl.BlockSpec((tm, k), lambda j, i: (i, 0)),
            pl.BlockSpec((k, tn), lambda j, i: (0, j)),
        ],
        out_specs=pl.BlockSpec((tm, tn), lambda j, i: (i, j)),
        out_shape=jax.ShapeDtypeStruct((m, n), BF16),
        compiler_params=_cparams(("arbitrary", "arbitrary"), 48),
        name="in_proj",
    )(a, w)


def _gelu_tanh(x):
    return 0.5 * x * (1.0 + jnp.tanh(0.7978845608028654 * (x + 0.044715 * (x * x * x))))


def _lru_kernel(x_ref, g_ref, cw_ref, cb_ref, wr_ref, br_ref, wi_ref, bi_ref, lam_ref,
                o_ref, tail_ref, carry_ref):
    @pl.when(pl.program_id(1) == 0)
    def _():
        tail_ref[...] = jnp.zeros_like(tail_ref)
        carry_ref[...] = jnp.zeros_like(carry_ref)

    ts, w = x_ref.shape
    groups = ts // SUBLANES
    x3 = x_ref[...].astype(F32).reshape(groups, SUBLANES, w)
    xe = jnp.concatenate([tail_ref[...][None], x3], axis=0)
    sub = lax.broadcasted_iota(jnp.int32, (groups, SUBLANES, w), 1)

    y = cb_ref[...][None] + cw_ref[CONV_WIDTH - 1:CONV_WIDTH, :][None] * x3
    for k in range(1, CONV_WIDTH):
        rolled = pltpu.roll(xe, k, axis=1)
        shifted = jnp.where(sub < k, rolled[:-1], rolled[1:])
        y = y + cw_ref[CONV_WIDTH - 1 - k:CONV_WIDTH - k, :][None] * shifted
    tail_ref[...] = x3[groups - 1]

    xa = y.reshape(ts, w)
    xab = xa.astype(BF16)

    def block_diag(w_ref, b_ref):
        parts = [
            jnp.dot(xab[:, n * LRU_BLOCK_W:(n + 1) * LRU_BLOCK_W], w_ref[n],
                    preferred_element_type=F32)
            for n in range(LRU_BLOCKS)
        ]
        return jnp.concatenate(parts, axis=1) + b_ref[...]

    r = jax.nn.sigmoid(block_diag(wr_ref, br_ref))
    i = jax.nn.sigmoid(block_diag(wi_ref, bi_ref))
    neg_lam = -lam_ref[...]
    softplus = jnp.maximum(neg_lam, 0.0) + jnp.log1p(jnp.exp(-jnp.abs(neg_lam)))
    a = jnp.exp((-LRU_C) * r * softplus)
    u = jnp.sqrt(1.0 - a * a) * (i * xa)

    a3 = a.reshape(groups, SUBLANES, w)
    u3 = u.reshape(groups, SUBLANES, w)
    d = 1
    while d < SUBLANES:
        a_prev = pltpu.roll(a3, d, axis=1)
        u_prev = pltpu.roll(u3, d, axis=1)
        take = sub >= d
        u3 = jnp.where(take, a3 * u_prev + u3, u3)
        a3 = jnp.where(take, a3 * a_prev, a3)
        d *= 2
    carry = carry_ref[...]
    hs = []
    for gi in range(groups):
        hg = u3[gi] + a3[gi] * carry
        hs.append(hg)
        carry = hg[SUBLANES - 1:SUBLANES, :]
    carry_ref[...] = carry
    h = jnp.stack(hs, axis=0).reshape(ts, w)

    o_ref[...] = (h * _gelu_tanh(g_ref[...].astype(F32))).astype(o_ref.dtype)


def _lru(proj, conv_w, conv_b, w_r, b_r, w_i, b_i, lam, batch, seq, ts=256):
    t = proj.shape[0]
    w = conv_w.shape[1]
    per_b = seq // ts
    row = lambda b, s: b * per_b + s
    vec = lambda v: v.reshape(1, w)
    const2 = lambda b, s: (0, 0)
    const3 = lambda b, s: (0, 0, 0)
    return pl.pallas_call(
        _lru_kernel,
        grid=(batch, per_b),
        in_specs=[
            pl.BlockSpec((ts, w), lambda b, s: (row(b, s), 0)),
            pl.BlockSpec((ts, w), lambda b, s: (row(b, s), 1)),
            pl.BlockSpec((CONV_WIDTH, w), const2),
            pl.BlockSpec((1, w), const2),
            pl.BlockSpec((LRU_BLOCKS, LRU_BLOCK_W, LRU_BLOCK_W), const3),
            pl.BlockSpec((1, w), const2),
            pl.BlockSpec((LRU_BLOCKS, LRU_BLOCK_W, LRU_BLOCK_W), const3),
            pl.BlockSpec((1, w), const2),
            pl.BlockSpec((1, w), const2),
        ],
        out_specs=pl.BlockSpec((ts, w), lambda b, s: (row(b, s), 0)),
        out_shape=jax.ShapeDtypeStruct((t, w), BF16),
        scratch_shapes=[pltpu.VMEM((SUBLANES, w), F32), pltpu.VMEM((1, w), F32)],
        compiler_params=_cparams(("arbitrary", "arbitrary"), 48),
        name="conv_rglru",
    )(proj, proj, conv_w, vec(conv_b), w_r.astype(BF16), vec(b_r), w_i.astype(BF16), vec(b_i),
      vec(lam))


def _rotary(x, cos_ref, sin_lo_ref, sin_hi_ref, rows):
    up = pltpu.roll(x, HEAD_DIM - ROT_HALF, axis=1)
    down = pltpu.roll(x, ROT_HALF, axis=1)
    return x * cos_ref[rows, :] + up * sin_lo_ref[rows, :] + down * sin_hi_ref[rows, :]


_NT = (((1,), (1,)), ((), ()))


def _moba_kernel(q_ref, k_ref, v_ref, cos_ref, sin_lo_ref, sin_hi_ref, o_ref, krot_ref, kmean_ref):
    qi = pl.program_id(2)
    seq = k_ref.shape[0]
    n_blk = seq // MOBA_BLOCK
    scale = HEAD_DIM ** -0.5

    @pl.when(qi == 0)
    def _():
        kmean_ref[...] = jnp.zeros_like(kmean_ref)
        for n in range(n_blk):
            rows = pl.ds(n * MOBA_BLOCK, MOBA_BLOCK)
            kr = _rotary(k_ref[rows, :].astype(F32), cos_ref, sin_lo_ref, sin_hi_ref, rows)
            krot_ref[rows, :] = kr.astype(krot_ref.dtype)
            kmean_ref[n:n + 1, :] = jnp.mean(kr, axis=0, keepdims=True)

    own_rows = pl.ds(pl.multiple_of(qi * MOBA_BLOCK, MOBA_BLOCK), MOBA_BLOCK)
    qb = _rotary(q_ref[...].astype(F32), cos_ref, sin_lo_ref, sin_hi_ref, own_rows).astype(BF16)

    gate = lax.dot_general(qb, kmean_ref[...].astype(BF16), _NT, preferred_element_type=F32)
    lane_i = lax.broadcasted_iota(jnp.int32, gate.shape, 1)
    lane = lane_i.astype(F32)
    past = lane_i < qi
    g = jnp.where(past, gate, -jnp.inf)
    sel = jnp.zeros_like(gate)
    for _ in range(MOBA_TOPK):
        best = jnp.max(g, axis=1, keepdims=True)
        idx = jnp.min(jnp.where(g == best, lane, float(LANES)), axis=1, keepdims=True)
        pick = lane == idx
        sel = jnp.where(pick, 1.0, sel)
        g = jnp.where(pick, -jnp.inf, g)
    sel = jnp.where(past, sel, 0.0)

    s = lax.dot_general(qb, krot_ref[own_rows, :], _NT, preferred_element_type=F32) * scale
    r_id = lax.broadcasted_iota(jnp.int32, s.shape, 0)
    c_id = lax.broadcasted_iota(jnp.int32, s.shape, 1)
    s = jnp.where(c_id <= r_id, s, NEG_INF)
    m0 = jnp.max(s, axis=1, keepdims=True)
    p = jnp.exp(s - m0)
    l0 = jnp.sum(p, axis=1, keepdims=True)
    acc0 = jnp.dot(p.astype(BF16), v_ref[own_rows, :], preferred_element_type=F32)

    def past_block(n, carry):
        m, l, acc = carry
        rows = pl.ds(pl.multiple_of(n * MOBA_BLOCK, MOBA_BLOCK), MOBA_BLOCK)
        s = lax.dot_general(qb, krot_ref[rows, :], _NT, preferred_element_type=F32) * scale
        chosen = jnp.sum(jnp.where(lane_i == n, sel, 0.0), axis=1, keepdims=True)
        s = jnp.where(chosen > 0.0, s, NEG_INF)
        m_new = jnp.maximum(m, jnp.max(s, axis=1, keepdims=True))
        alpha = jnp.exp(m - m_new)
        p = jnp.exp(s - m_new)
        l = alpha * l + jnp.sum(p, axis=1, keepdims=True)
        acc = alpha * acc + jnp.dot(p.astype(BF16), v_ref[rows, :], preferred_element_type=F32)
        return m_new, l, acc

    _, l, acc = lax.fori_loop(0, qi, past_block, (m0, l0, acc0))
    o_ref[...] = (acc / l).astype(o_ref.dtype)


def _rope_tables(seq):
    pos = jnp.arange(seq, dtype=F32)
    inv_freq = ROPE_THETA ** (-jnp.arange(0, ROT_DIM, 2, dtype=F32) / ROT_DIM)
    ang = pos[:, None] * inv_freq[None, :]
    cos, sin = jnp.cos(ang), jnp.sin(ang)
    rest = HEAD_DIM - ROT_DIM
    cos_t = jnp.concatenate([cos, cos, jnp.ones((seq, rest), F32)], axis=1)
    sin_lo = jnp.concatenate([-sin, jnp.zeros((seq, HEAD_DIM - ROT_HALF), F32)], axis=1)
    sin_hi = jnp.concatenate([jnp.zeros((seq, ROT_HALF), F32), sin, jnp.zeros((seq, rest), F32)], axis=1)
    return cos_t, sin_lo, sin_hi


def _moba(proj, tables, batch, seq, heads, q_col, k_col, v_col):
    t = proj.shape[0]
    n_blk = seq // MOBA_BLOCK
    table_spec = pl.BlockSpec((seq, HEAD_DIM), lambda b, h, qi: (0, 0))
    return pl.pallas_call(
        _moba_kernel,
        grid=(batch, heads, n_blk),
        in_specs=[
            pl.BlockSpec((MOBA_BLOCK, HEAD_DIM), lambda b, h, qi: (b * n_blk + qi, q_col + h)),
            pl.BlockSpec((seq, HEAD_DIM), lambda b, h, qi: (b, k_col + h)),
            pl.BlockSpec((seq, HEAD_DIM), lambda b, h, qi: (b, v_col + h)),
            table_spec, table_spec, table_spec,
        ],
        out_specs=pl.BlockSpec((MOBA_BLOCK, HEAD_DIM), lambda b, h, qi: (b * n_blk + qi, h)),
        out_shape=jax.ShapeDtypeStruct((t, heads * HEAD_DIM), BF16),
        scratch_shapes=[pltpu.VMEM((seq, HEAD_DIM), BF16), pltpu.VMEM((LANES, HEAD_DIM), F32)],
        compiler_params=_cparams(("arbitrary", "arbitrary", "arbitrary"), 48),
        name="moba_attention",
    )(proj, proj, proj, *tables)


def _deepnorm_ln(x, y, ep, ln_g, ln_b, alpha):
    r = alpha * x + (1.0 + ep[0:1]) * y
    mu = jnp.mean(r, axis=-1, keepdims=True)
    rc = r - mu
    var = jnp.mean(rc * rc, axis=-1, keepdims=True)
    return rc * lax.rsqrt(var + LN_EPS) * ln_g + ln_b


def _next_h(x_new, ep):
    return x_new * (1.0 + ep[1:2]) + ep[2:3]


def _mix_kernel(alpha, ya_ref, at_ref, ga0_ref, ga1_ref, gb0_ref, gb1_ref, x_ref, ep_ref,
                wpa_ref, wpb_ref, wout_ref, lng_ref, lnb_ref, xo_ref, ho_ref):
    y_a = jnp.dot(ya_ref[...], wpa_ref[...], preferred_element_type=F32)
    y_b = jnp.dot(at_ref[...], wpb_ref[...], preferred_element_type=F32)
    gate_a = jnp.concatenate([ga0_ref[...], ga1_ref[...]], axis=1).astype(F32)
    gate_b = jnp.concatenate([gb0_ref[...], gb1_ref[...]], axis=1).astype(F32)
    merged = jax.nn.sigmoid(gate_a) * y_a + jax.nn.sigmoid(gate_b) * y_b
    y = jnp.dot(merged.astype(BF16), wout_ref[...], preferred_element_type=F32)
    ep = ep_ref[0]
    x_new = _deepnorm_ln(x_ref[...], y, ep, lng_ref[...], lnb_ref[...], alpha)
    xo_ref[...] = x_new
    ho_ref[...] = _next_h(x_new, ep).astype(ho_ref.dtype)


def _mix(ya, attn, proj, x2d, ep, w_pa, w_pb, w_out, ln_g, ln_b, alpha, seq, ga_col, gb_col, tm=256):
    t, d = x2d.shape
    wa = ya.shape[1]
    half = d // 2
    per_b = seq // tm
    const = lambda i: (0, 0)
    return pl.pallas_call(
        functools.partial(_mix_kernel, alpha),
        grid=(t // tm,),
        in_specs=[
            pl.BlockSpec((tm, wa), lambda i: (i, 0)),
            pl.BlockSpec((tm, wa), lambda i: (i, 0)),
            pl.BlockSpec((tm, half), lambda i: (i, ga_col)),
            pl.BlockSpec((tm, half), lambda i: (i, ga_col + 1)),
            pl.BlockSpec((tm, half), lambda i: (i, gb_col)),
            pl.BlockSpec((tm, half), lambda i: (i, gb_col + 1)),
            pl.BlockSpec((tm, d), lambda i: (i, 0)),
            pl.BlockSpec((1, 3, d), lambda i: (i // per_b, 0, 0)),
            _resident((wa, d), const),
            _resident((wa, d), const),
            _resident((d, d), const),
            _resident((1, d), const),
            _resident((1, d), const),
        ],
        out_specs=[pl.BlockSpec((tm, d), lambda i: (i, 0)), pl.BlockSpec((tm, d), lambda i: (i, 0))],
        out_shape=[jax.ShapeDtypeStruct((t, d), F32), jax.ShapeDtypeStruct((t, d), BF16)],
        compiler_params=_cparams(("arbitrary",), 56),
        name="mix_out_ln",
    )(ya, attn, proj, proj, proj, proj, x2d, ep, w_pa, w_pb, w_out, ln_g.reshape(1, d),
      ln_b.reshape(1, d))


def _swiglu_step(h, wg, wu, wd):
    g = jnp.dot(h, wg, preferred_element_type=F32)
    u = jnp.dot(h, wu, preferred_element_type=F32)
    return jnp.dot((g * jax.nn.sigmoid(g) * u).astype(BF16), wd, preferred_element_type=F32)


def _ffn_kernel(alpha, h_ref, wg_ref, wu_ref, wd_ref, x_ref, ep_ref, lng_ref, lnb_ref,
                xo_ref, ho_ref, acc_ref):
    f = pl.program_id(1)

    @pl.when(f == 0)
    def _():
        acc_ref[...] = jnp.zeros_like(acc_ref)

    acc_ref[...] += _swiglu_step(h_ref[...], wg_ref[...], wu_ref[...], wd_ref[...])

    @pl.when(f == pl.num_programs(1) - 1)
    def _():
        ep = ep_ref[0]
        x_new = _deepnorm_ln(x_ref[...], acc_ref[...], ep, lng_ref[...], lnb_ref[...], alpha)
        xo_ref[...] = x_new
        ho_ref[...] = _next_h(x_new, ep).astype(ho_ref.dtype)


def _ffn(h, x2d, ep, w_gate, w_up, w_down, ln_g, ln_b, alpha, seq, tm=512, tf=512):
    t, d = x2d.shape
    ff = w_gate.shape[1]
    per_b = seq // tm
    return pl.pallas_call(
        functools.partial(_ffn_kernel, alpha),
        grid=(t // tm, ff // tf),
        in_specs=[
            pl.BlockSpec((tm, d), lambda i, f: (i, 0)),
            pl.BlockSpec((d, tf), lambda i, f: (0, f)),
            pl.BlockSpec((d, tf), lambda i, f: (0, f)),
            pl.BlockSpec((tf, d), lambda i, f: (f, 0)),
            pl.BlockSpec((tm, d), lambda i, f: (i, 0)),
            pl.BlockSpec((1, 3, d), lambda i, f: (i // per_b, 0, 0)),
            pl.BlockSpec((1, d), lambda i, f: (0, 0)),
            pl.BlockSpec((1, d), lambda i, f: (0, 0)),
        ],
        out_specs=[pl.BlockSpec((tm, d), lambda i, f: (i, 0)), pl.BlockSpec((tm, d), lambda i, f: (i, 0))],
        out_shape=[jax.ShapeDtypeStruct((t, d), F32), jax.ShapeDtypeStruct((t, d), BF16)],
        scratch_shapes=[pltpu.VMEM((tm, d), F32)],
        compiler_params=_cparams(("arbitrary", "arbitrary"), 56),
        name="ffn_dense_ln",
    )(h, w_gate, w_up, w_down, x2d, ep, ln_g.reshape(1, d), ln_b.reshape(1, d))


def _router_kernel(h_ref, w_ref, b_ref, o_ref):
    logits = jnp.dot(h_ref[...], w_ref[...], preferred_element_type=F32) + b_ref[...]
    lane_i = lax.broadcasted_iota(jnp.int32, logits.shape, 1)
    lane = lane_i.astype(F32)
    lg = jnp.where(lane_i < N_EXPERTS, logits, -jnp.inf)
    m1 = jnp.max(lg, axis=1, keepdims=True)
    i1 = jnp.min(jnp.where(lg == m1, lane, float(LANES)), axis=1, keepdims=True)
    lg2 = jnp.where(lane == i1, -jnp.inf, lg)
    m2 = jnp.max(lg2, axis=1, keepdims=True)
    i2 = jnp.min(jnp.where(lg2 == m2, lane, float(LANES)), axis=1, keepdims=True)
    e2 = jnp.exp(m2 - m1)
    denom = 1.0 + e2
    out = jnp.where(lane_i == 0, i1, 0.0)
    out = jnp.where(lane_i == 1, i2, out)
    out = jnp.where(lane_i == 2, 1.0 / denom, out)
    out = jnp.where(lane_i == 3, e2 / denom, out)
    o_ref[...] = out


def _router(h, w_router_pad, b_router_pad, tm=1024):
    t, d = h.shape
    return pl.pallas_call(
        _router_kernel,
        grid=(t // tm,),
        in_specs=[
            pl.BlockSpec((tm, d), lambda i: (i, 0)),
            pl.BlockSpec((d, LANES), lambda i: (0, 0)),
            pl.BlockSpec((1, LANES), lambda i: (0, 0)),
        ],
        out_specs=pl.BlockSpec((tm, LANES), lambda i: (i, 0)),
        out_shape=jax.ShapeDtypeStruct((t, LANES), F32),
        compiler_params=_cparams(("arbitrary",), 32),
        name="moe_router",
    )(h, w_router_pad, b_router_pad)


def _gmm_kernel(te_ref, tv_ref, xs_ref, wg_ref, wu_ref, wd_ref, rw_ref, o_ref, acc_ref):
    i = pl.program_id(0)
    f = pl.program_id(1)
    last = pl.num_programs(1) - 1
    valid = tv_ref[i] > 0

    @pl.when(jnp.logical_and(valid, f == 0))
    def _():
        acc_ref[...] = jnp.zeros_like(acc_ref)

    @pl.when(valid)
    def _():
        acc_ref[...] += _swiglu_step(xs_ref[...], wg_ref[0], wu_ref[0], wd_ref[0])

    @pl.when(jnp.logical_and(valid, f == last))
    def _():
        o_ref[...] = (acc_ref[...] * rw_ref[...]).astype(o_ref.dtype)

    @pl.when(jnp.logical_and(jnp.logical_not(valid), f == last))
    def _():
        o_ref[...] = jnp.zeros_like(o_ref)


def _gmm(tile_expert, tile_valid, xs, row_w, w_gate, w_up, w_down, tm, tf=512):
    r, d = xs.shape
    ff = w_gate.shape[2]
    nf = ff // tf

    def f_idx(i, f, tv):
        return jnp.where(tv[i] > 0, f, nf - 1)

    grid_spec = pltpu.PrefetchScalarGridSpec(
        num_scalar_prefetch=2,
        grid=(r // tm, nf),
        in_specs=[
            pl.BlockSpec((tm, d), lambda i, f, te, tv: (i, 0)),
            pl.BlockSpec((1, d, tf), lambda i, f, te, tv: (te[i], 0, f_idx(i, f, tv))),
            pl.BlockSpec((1, d, tf), lambda i, f, te, tv: (te[i], 0, f_idx(i, f, tv))),
            pl.BlockSpec((1, tf, d), lambda i, f, te, tv: (te[i], f_idx(i, f, tv), 0)),
            pl.BlockSpec((tm, 1), lambda i, f, te, tv: (i, 0)),
        ],
        out_specs=pl.BlockSpec((tm, d), lambda i, f, te, tv: (i, 0)),
        scratch_shapes=[pltpu.VMEM((tm, d), F32)],
    )
    return pl.pallas_call(
        _gmm_kernel,
        grid_spec=grid_spec,
        out_shape=jax.ShapeDtypeStruct((r, d), BF16),
        compiler_params=_cparams(("arbitrary", "arbitrary"), 56),
        name="moe_grouped_swiglu",
    )(tile_expert, tile_valid, xs, w_gate, w_up, w_down, row_w)


def _ln_kernel(alpha, y_ref, x_ref, ep_ref, lng_ref, lnb_ref, xo_ref, ho_ref):
    ep = ep_ref[0]
    y = y_ref[:, 0, :].astype(F32) + y_ref[:, 1, :].astype(F32)
    x_new = _deepnorm_ln(x_ref[...], y, ep, lng_ref[...], lnb_ref[...], alpha)
    xo_ref[...] = x_new
    ho_ref[...] = _next_h(x_new, ep).astype(ho_ref.dtype)


def _ln(y_pairs, x2d, ep, ln_g, ln_b, alpha, seq, tm=256):
    t, d = x2d.shape
    per_b = seq // tm
    return pl.pallas_call(
        functools.partial(_ln_kernel, alpha),
        grid=(t // tm,),
        in_specs=[
            pl.BlockSpec((tm, 2, d), lambda i: (i, 0, 0)),
            pl.BlockSpec((tm, d), lambda i: (i, 0)),
            pl.BlockSpec((1, 3, d), lambda i: (i // per_b, 0, 0)),
            pl.BlockSpec((1, d), lambda i: (0, 0)),
            pl.BlockSpec((1, d), lambda i: (0, 0)),
        ],
        out_specs=[pl.BlockSpec((tm, d), lambda i: (i, 0)), pl.BlockSpec((tm, d), lambda i: (i, 0))],
        out_shape=[jax.ShapeDtypeStruct((t, d), F32), jax.ShapeDtypeStruct((t, d), BF16)],
        compiler_params=_cparams(("arbitrary",), 48),
        name="moe_combine_ln",
    )(y_pairs, x2d, ep, ln_g.reshape(1, d), ln_b.reshape(1, d))


def _dispatch_plan(route, tm):
    t = route.shape[0]
    n_pairs = 2 * t
    n_rows = n_pairs + N_EXPERTS * tm
    expert = route[:, :2].astype(jnp.int32).reshape(n_pairs)
    weight = route[:, 2:4].reshape(n_pairs)
    onehot = (expert[:, None] == jnp.arange(N_EXPERTS, dtype=jnp.int32)[None, :]).astype(jnp.int32)
    rank = jnp.take_along_axis(jnp.cumsum(onehot, axis=0) - onehot, expert[:, None], axis=1)[:, 0]
    counts = jnp.sum(onehot, axis=0)
    padded = ((counts + tm - 1) // tm) * tm
    ends = jnp.cumsum(padded)
    starts = ends - padded
    pos = starts[expert] + rank
    src_token = jnp.zeros((n_rows,), jnp.int32).at[pos].set(jnp.arange(n_pairs, dtype=jnp.int32) // 2)
    row_w = jnp.zeros((n_rows,), F32).at[pos].set(weight)
    tile_start = jnp.arange(n_rows // tm, dtype=jnp.int32) * tm
    tile_valid = (tile_start < ends[-1]).astype(jnp.int32)
    tile_expert = jnp.minimum(jnp.searchsorted(ends, tile_start, side="right"), N_EXPERTS - 1).astype(jnp.int32)
    last_valid_expert = jnp.max(jnp.where(tile_valid > 0, tile_expert, 0))
    tile_expert = jnp.where(tile_valid > 0, tile_expert, last_valid_expert)
    return pos, src_token, row_w.reshape(n_rows, 1), tile_expert, tile_valid


def kernel(x, c, w_in, conv_w, conv_b, w_rgate, b_rgate, w_igate, b_igate, lru_lambda, w_proj_a, w_proj_b, w_out, w_ada, b_ada, ln_mix_g, ln_mix_b, ln_ffn_g, ln_ffn_b, ffn_w_gate, ffn_w_up, ffn_w_down, moe_w_router, moe_b_router, moe_w_gate, moe_w_up, moe_w_down):
    batch, seq, d = x.shape
    depth = w_in.shape[0]
    t = batch * seq
    lru_w = conv_w.shape[2]
    attn_w = w_proj_b.shape[1]
    heads = attn_w // HEAD_DIM
    alpha = (2.0 * depth) ** 0.25
    moe_tm = 1024

    c_pad = jnp.zeros((SUBLANES, d), F32).at[:batch].set(c)
    ada = _ada_all_layers(c_pad, w_ada, b_ada)[:, :batch].reshape(depth, batch, 6, d)
    zeros_bd = jnp.zeros((batch, d), F32)

    def epilogue_params(gate, scale, shift):
        return jnp.stack([gate, scale, shift], axis=1)

    tables = _rope_tables(seq)
    x2d = x.reshape(t, d)
    h = _modulate(x2d, epilogue_params(zeros_bd, ada[0, :, 1], ada[0, :, 0]), seq)

    q_col = 2 * lru_w // HEAD_DIM
    k_col = q_col + heads
    v_col = k_col + heads
    ga_col = (2 * lru_w + 3 * attn_w) // (d // 2)
    gb_col = ga_col + 2

    for l in range(depth):
        sh1, sc1, g1, sh2, sc2, g2 = (ada[l, :, j] for j in range(6))
        proj = _matmul(h, w_in[l].astype(BF16))
        ya = _lru(proj, conv_w[l], conv_b[l], w_rgate[l], b_rgate[l], w_igate[l], b_igate[l],
                  lru_lambda[l], batch, seq)
        attn = _moba(proj, tables, batch, seq, heads, q_col, k_col, v_col)
        x2d, h = _mix(ya, attn, proj, x2d, epilogue_params(g1, sc2, sh2),
                      w_proj_a[l].astype(BF16), w_proj_b[l].astype(BF16), w_out[l].astype(BF16),
                      ln_mix_g[l], ln_mix_b[l], alpha, seq, ga_col, gb_col)
        if l + 1 < depth:
            ep = epilogue_params(g2, ada[l + 1, :, 1], ada[l + 1, :, 0])
        else:
            ep = epilogue_params(g2, zeros_bd, zeros_bd)
        j = l // 2
        if l % 2 == 0:
            x2d, h = _ffn(h, x2d, ep, ffn_w_gate[j].astype(BF16), ffn_w_up[j].astype(BF16),
                          ffn_w_down[j].astype(BF16), ln_ffn_g[l], ln_ffn_b[l], alpha, seq)
        else:
            w_r_pad = jnp.zeros((d, LANES), BF16).at[:, :N_EXPERTS].set(moe_w_router[j].astype(BF16))
            b_r_pad = jnp.zeros((1, LANES), F32).at[0, :N_EXPERTS].set(moe_b_router[j])
            route = _router(h, w_r_pad, b_r_pad)
            pos, src_token, row_w, tile_expert, tile_valid = _dispatch_plan(route, moe_tm)
            xs = jnp.take(h, src_token, axis=0)
            ys = _gmm(tile_expert, tile_valid, xs, row_w, moe_w_gate[j].astype(BF16),
                      moe_w_up[j].astype(BF16), moe_w_down[j].astype(BF16), moe_tm)
            y_pairs = jnp.take(ys, pos, axis=0).reshape(t, 2, d)
            x2d, h = _ln(y_pairs, x2d, ep, ln_ffn_g[l], ln_ffn_b[l], alpha, seq)
    return x2d.reshape(batch, seq, d)
```

```python
import functools

import jax
import jax.numpy as jnp
from jax import lax
from jax.experimental import pallas as pl
from jax.experimental.pallas import tpu as pltpu

F32 = jnp.float32
BF16 = jnp.bfloat16

LRU_BLOCKS = 8
LRU_BLOCK_W = 128
CONV_WIDTH = 4
LRU_C = 8.0
HEAD_DIM = 128
MOBA_BLOCK = 256
MOBA_TOPK = 3
MOBA_HEADS_PER_STEP = 2
MOBA_SWEEP = 4
ROPE_THETA = 500000.0
ROT_DIM = HEAD_DIM // 4
ROT_HALF = ROT_DIM // 2
NEG_INF = -1e30
N_EXPERTS = 8
LN_EPS = 1e-5
LANES = 128
SUBLANES = 8
MIB = 1024 * 1024


def _cparams(semantics, vmem_mib):
    return pltpu.CompilerParams(dimension_semantics=semantics, vmem_limit_bytes=vmem_mib * MIB)


def _resident(block_shape, index_map):
    return pl.BlockSpec(block_shape, index_map, pipeline_mode=pl.Buffered(1))


def _ada_kernel(c_ref, w_ref, b_ref, o_ref):
    c = c_ref[...]
    c_act = c * jax.nn.sigmoid(c)
    o_ref[0] = jnp.dot(c_act, w_ref[0], preferred_element_type=F32) + b_ref[0]


def _ada_all_layers(c_pad, w_ada, b_ada, tn=1024):
    depth, d, n = w_ada.shape
    rows = c_pad.shape[0]
    return pl.pallas_call(
        _ada_kernel,
        grid=(depth, n // tn),
        in_specs=[
            pl.BlockSpec((rows, d), lambda l, j: (0, 0)),
            pl.BlockSpec((1, d, tn), lambda l, j: (l, 0, j)),
            pl.BlockSpec((1, 1, tn), lambda l, j: (l, 0, j)),
        ],
        out_specs=pl.BlockSpec((1, rows, tn), lambda l, j: (l, 0, j)),
        out_shape=jax.ShapeDtypeStruct((depth, rows, n), F32),
        compiler_params=_cparams(("arbitrary", "arbitrary"), 40),
        name="ada",
    )(c_pad, w_ada, b_ada.reshape(depth, 1, n))


def _modulate_kernel(x_ref, ep_ref, h_ref):
    ep = ep_ref[0]
    h_ref[...] = (x_ref[...] * (1.0 + ep[1:2]) + ep[2:3]).astype(h_ref.dtype)


def _modulate(x2d, ep, seq, tm=512):
    t, d = x2d.shape
    per_b = seq // tm
    return pl.pallas_call(
        _modulate_kernel,
        grid=(t // tm,),
        in_specs=[
            pl.BlockSpec((tm, d), lambda i: (i, 0)),
            pl.BlockSpec((1, 3, d), lambda i: (i // per_b, 0, 0)),
        ],
        out_specs=pl.BlockSpec((tm, d), lambda i: (i, 0)),
        out_shape=jax.ShapeDtypeStruct((t, d), BF16),
        compiler_params=_cparams(("arbitrary",), 32),
        name="modulate",
    )(x2d, ep)


def _matmul_kernel(a_ref, w_ref, o_ref):
    o_ref[...] = jnp.dot(a_ref[...], w_ref[...], preferred_element_type=F32).astype(o_ref.dtype)


def _matmul(a, w, tm=1024, tn=1024):
    m, k = a.shape
    n = w.shape[1]
    return pl.pallas_call(
        _matmul_kernel,
        grid=(n // tn, m // tm),
        in_specs=[
            pl.BlockSpec((tm, k), lambda j, i: (i, 0)),
            pl.BlockSpec((k, tn), lambda j, i: (0, j)),
        ],
        out_specs=pl.BlockSpec((tm, tn), lambda j, i: (i, j)),
        out_shape=jax.ShapeDtypeStruct((m, n), BF16),
        compiler_params=_cparams(("arbitrary", "arbitrary"), 48),
        name="in_proj",
    )(a, w)


def _gelu_tanh(x):
    return 0.5 * x * (1.0 + jnp.tanh(0.7978845608028654 * (x + 0.044715 * (x * x * x))))


def _lru_kernel(x_ref, g_ref, cw_ref, cb_ref, wr_ref, br_ref, wi_ref, bi_ref, lam_ref,
                o_ref, tail_ref, carry_ref):
    @pl.when(pl.program_id(1) == 0)
    def _():
        tail_ref[...] = jnp.zeros_like(tail_ref)
        carry_ref[...] = jnp.zeros_like(carry_ref)

    ts, w = x_ref.shape
    groups = ts // SUBLANES
    x3 = x_ref[...].astype(F32).reshape(groups, SUBLANES, w)
    xe = jnp.concatenate([tail_ref[...][None], x3], axis=0)
    sub = lax.broadcasted_iota(jnp.int32, (groups, SUBLANES, w), 1)

    y = cb_ref[...][None] + cw_ref[CONV_WIDTH - 1:CONV_WIDTH, :][None] * x3
    for k in range(1, CONV_WIDTH):
        rolled = pltpu.roll(xe, k, axis=1)
        shifted = jnp.where(sub < k, rolled[:-1], rolled[1:])
        y = y + cw_ref[CONV_WIDTH - 1 - k:CONV_WIDTH - k, :][None] * shifted
    tail_ref[...] = x3[groups - 1]

    xa = y.reshape(ts, w)
    xab = xa.astype(BF16)

    def block_diag(w_ref, b_ref):
        parts = [
            jnp.dot(xab[:, n * LRU_BLOCK_W:(n + 1) * LRU_BLOCK_W], w_ref[n],
                    preferred_element_type=F32)
            for n in range(LRU_BLOCKS)
        ]
        return jnp.concatenate(parts, axis=1) + b_ref[...]

    r = jax.nn.sigmoid(block_diag(wr_ref, br_ref))
    i = jax.nn.sigmoid(block_diag(wi_ref, bi_ref))
    neg_lam = -lam_ref[...]
    softplus = jnp.maximum(neg_lam, 0.0) + jnp.log1p(jnp.exp(-jnp.abs(neg_lam)))
    a = jnp.exp((-LRU_C) * r * softplus)
    u = jnp.sqrt(1.0 - a * a) * (i * xa)

    a3 = a.reshape(groups, SUBLANES, w)
    u3 = u.reshape(groups, SUBLANES, w)
    d = 1
    while d < SUBLANES:
        a_prev = pltpu.roll(a3, d, axis=1)
        u_prev = pltpu.roll(u3, d, axis=1)
        take = sub >= d
        u3 = jnp.where(take, a3 * u_prev + u3, u3)
        a3 = jnp.where(take, a3 * a_prev, a3)
        d *= 2
    carry = carry_ref[...]
    hs = []
    for gi in range(groups):
        hg = u3[gi] + a3[gi] * carry
        hs.append(hg)
        carry = hg[SUBLANES - 1:SUBLANES, :]
    carry_ref[...] = carry
    h = jnp.stack(hs, axis=0).reshape(ts, w)

    o_ref[...] = (h * _gelu_tanh(g_ref[...].astype(F32))).astype(o_ref.dtype)


def _lru(proj, conv_w, conv_b, w_r, b_r, w_i, b_i, lam, batch, seq, ts=256):
    t = proj.shape[0]
    w = conv_w.shape[1]
    per_b = seq // ts
    row = lambda b, s: b * per_b + s
    vec = lambda v: v.reshape(1, w)
    const2 = lambda b, s: (0, 0)
    const3 = lambda b, s: (0, 0, 0)
    return pl.pallas_call(
        _lru_kernel,
        grid=(batch, per_b),
        in_specs=[
            pl.BlockSpec((ts, w), lambda b, s: (row(b, s), 0)),
            pl.BlockSpec((ts, w), lambda b, s: (row(b, s), 1)),
            pl.BlockSpec((CONV_WIDTH, w), const2),
            pl.BlockSpec((1, w), const2),
            pl.BlockSpec((LRU_BLOCKS, LRU_BLOCK_W, LRU_BLOCK_W), const3),
            pl.BlockSpec((1, w), const2),
            pl.BlockSpec((LRU_BLOCKS, LRU_BLOCK_W, LRU_BLOCK_W), const3),
            pl.BlockSpec((1, w), const2),
            pl.BlockSpec((1, w), const2),
        ],
        out_specs=pl.BlockSpec((ts, w), lambda b, s: (row(b, s), 0)),
        out_shape=jax.ShapeDtypeStruct((t, w), BF16),
        scratch_shapes=[pltpu.VMEM((SUBLANES, w), F32), pltpu.VMEM((1, w), F32)],
        compiler_params=_cparams(("arbitrary", "arbitrary"), 48),
        name="conv_rglru",
    )(proj, proj, conv_w, vec(conv_b), w_r.astype(BF16), vec(b_r), w_i.astype(BF16), vec(b_i),
      vec(lam))


def _rotary(x, cos_ref, sin_lo_ref, sin_hi_ref, rows):
    up = pltpu.roll(x, HEAD_DIM - ROT_HALF, axis=1)
    down = pltpu.roll(x, ROT_HALF, axis=1)
    return x * cos_ref[rows, :] + up * sin_lo_ref[rows, :] + down * sin_hi_ref[rows, :]


_NT = (((1,), (1,)), ((), ()))


def _moba_kernel(q_ref, k_ref, v_ref, cos_ref, sin_lo_ref, sin_hi_ref, o_ref,
                 krot_ref, vt_ref, vtg_ref, kmean_ref, sel_ref, raw_ref):
    qi = pl.program_id(2)
    n_heads, n_blk = kmean_ref.shape[0], kmean_ref.shape[1]
    scale2 = HEAD_DIM ** -0.5 * 1.4426950408889634
    head_cols = [slice(hh * HEAD_DIM, (hh + 1) * HEAD_DIM) for hh in range(n_heads)]

    @pl.when(qi == 0)
    def _():
        for hh in range(n_heads):
            for n in range(n_blk):
                rows = pl.ds(n * MOBA_BLOCK, MOBA_BLOCK)
                kr = _rotary(k_ref[rows, head_cols[hh]].astype(F32), cos_ref, sin_lo_ref,
                             sin_hi_ref, rows)
                krot_ref[hh, rows, :] = kr.astype(krot_ref.dtype)
                kmean_ref[hh, n:n + 1, :] = jnp.mean(kr, axis=0, keepdims=True)
                vt = v_ref[rows, head_cols[hh]].astype(F32).T.astype(vt_ref.dtype)
                vt_ref[hh, n] = vt
                c = n % MOBA_SWEEP
                vtg_ref[hh, n // MOBA_SWEEP, :, c * MOBA_BLOCK:(c + 1) * MOBA_BLOCK] = vt

    own_rows = pl.ds(pl.multiple_of(qi * MOBA_BLOCK, MOBA_BLOCK), MOBA_BLOCK)

    qts, init = [], []
    for hh in range(n_heads):
        qt = _rotary(q_ref[:, head_cols[hh]].astype(F32), cos_ref, sin_lo_ref, sin_hi_ref,
                     own_rows).T.astype(BF16)
        qts.append(qt)

        gate = jnp.dot(kmean_ref[hh].astype(BF16), qt, preferred_element_type=F32)
        blk_i = lax.broadcasted_iota(jnp.int32, gate.shape, 0)
        blk = blk_i.astype(F32)
        past = blk_i < qi
        g = jnp.where(past, gate, -jnp.inf)
        sel = jnp.zeros_like(gate)
        for _ in range(MOBA_TOPK):
            best = jnp.max(g, axis=0, keepdims=True)
            idx = jnp.min(jnp.where(g == best, blk, float(n_blk)), axis=0, keepdims=True)
            pick = blk == idx
            sel = jnp.where(pick, 1.0, sel)
            g = jnp.where(pick, -jnp.inf, g)
        sel_ref[hh] = jnp.where(past, sel, 0.0)

        s = jnp.dot(krot_ref[hh, own_rows, :], qt, preferred_element_type=F32) * scale2
        key_id = lax.broadcasted_iota(jnp.int32, s.shape, 0)
        qry_id = lax.broadcasted_iota(jnp.int32, s.shape, 1)
        s = jnp.where(key_id <= qry_id, s, NEG_INF)
        m0 = jnp.max(s, axis=0, keepdims=True)
        p = jnp.exp2(s - m0)
        l0 = jnp.sum(p, axis=0, keepdims=True)
        acc0 = jnp.dot(vt_ref[hh, qi], p.astype(BF16), preferred_element_type=F32)
        init.append((m0, l0, acc0))

    span = MOBA_SWEEP * MOBA_BLOCK
    last_group = n_blk // MOBA_SWEEP - 1

    def raw_scores(hh, j):
        rows = pl.ds(pl.multiple_of(j * span, span), span)
        return jnp.dot(krot_ref[hh, rows, :], qts[hh], preferred_element_type=F32)

    for hh in range(n_heads):
        raw_ref[hh] = raw_scores(hh, 0)

    def past_group(j, carry):
        out = []
        for hh in range(n_heads):
            m, l, acc = carry[hh]
            s = jnp.concatenate([
                jnp.where(sel_ref[hh, pl.ds(j * MOBA_SWEEP + c, 1), :] > 0.0,
                          raw_ref[hh, c * MOBA_BLOCK:(c + 1) * MOBA_BLOCK, :] * scale2, NEG_INF)
                for c in range(MOBA_SWEEP)], axis=0)
            raw_ref[hh] = raw_scores(hh, jnp.minimum(j + 1, last_group))
            m_new = jnp.maximum(m, jnp.max(s, axis=0, keepdims=True))
            alpha = jnp.exp2(m - m_new)
            p = jnp.exp2(s - m_new)
            l = alpha * l + jnp.sum(p, axis=0, keepdims=True)
            acc = alpha * acc + jnp.dot(vtg_ref[hh, j], p.astype(BF16), preferred_element_type=F32)
            out.append((m_new, l, acc))
        return tuple(out)

    n_groups = lax.div(qi + (MOBA_SWEEP - 1), MOBA_SWEEP)
    final = lax.fori_loop(0, n_groups, past_group, tuple(init))
    for hh in range(n_heads):
        _, l, acc = final[hh]
        o_ref[:, head_cols[hh]] = (acc / l).T.astype(o_ref.dtype)


def _rope_tables(seq):
    pos = jnp.arange(seq, dtype=F32)
    inv_freq = ROPE_THETA ** (-jnp.arange(0, ROT_DIM, 2, dtype=F32) / ROT_DIM)
    ang = pos[:, None] * inv_freq[None, :]
    cos, sin = jnp.cos(ang), jnp.sin(ang)
    rest = HEAD_DIM - ROT_DIM
    cos_t = jnp.concatenate([cos, cos, jnp.ones((seq, rest), F32)], axis=1)
    sin_lo = jnp.concatenate([-sin, jnp.zeros((seq, HEAD_DIM - ROT_HALF), F32)], axis=1)
    sin_hi = jnp.concatenate([jnp.zeros((seq, ROT_HALF), F32), sin, jnp.zeros((seq, rest), F32)], axis=1)
    return cos_t, sin_lo, sin_hi


def _moba(proj, tables, batch, seq, heads, q_col, k_col, v_col, hg=MOBA_HEADS_PER_STEP):
    t = proj.shape[0]
    n_blk = seq // MOBA_BLOCK
    gw = hg * HEAD_DIM
    table_spec = _resident((seq, HEAD_DIM), lambda b, h, qi: (0, 0))
    return pl.pallas_call(
        _moba_kernel,
        grid=(batch, heads // hg, n_blk),
        in_specs=[
            pl.BlockSpec((MOBA_BLOCK, gw), lambda b, h, qi: (b * n_blk + qi, q_col + h)),
            pl.BlockSpec((seq, gw), lambda b, h, qi: (b, k_col + h)),
            pl.BlockSpec((seq, gw), lambda b, h, qi: (b, v_col + h)),
            table_spec, table_spec, table_spec,
        ],
        out_specs=pl.BlockSpec((MOBA_BLOCK, gw), lambda b, h, qi: (b * n_blk + qi, h)),
        out_shape=jax.ShapeDtypeStruct((t, heads * HEAD_DIM), BF16),
        scratch_shapes=[
            pltpu.VMEM((hg, seq, HEAD_DIM), BF16),
            pltpu.VMEM((hg, n_blk, HEAD_DIM, MOBA_BLOCK), BF16),
            pltpu.VMEM((hg, n_blk // MOBA_SWEEP, HEAD_DIM, MOBA_SWEEP * MOBA_BLOCK), BF16),
            pltpu.VMEM((hg, n_blk, HEAD_DIM), F32),
            pltpu.VMEM((hg, n_blk, MOBA_BLOCK), F32),
            pltpu.VMEM((hg, MOBA_SWEEP * MOBA_BLOCK, MOBA_BLOCK), F32),
        ],
        compiler_params=_cparams(("arbitrary", "arbitrary", "arbitrary"), 48),
        name="moba_attention",
    )(proj, proj, proj, *tables)


def _deepnorm_ln(x, y, ep, ln_g, ln_b, alpha):
    r = alpha * x + (1.0 + ep[0:1]) * y
    mu = jnp.mean(r, axis=-1, keepdims=True)
    rc = r - mu
    var = jnp.mean(rc * rc, axis=-1, keepdims=True)
    return rc * lax.rsqrt(var + LN_EPS) * ln_g + ln_b


def _next_h(x_new, ep):
    return x_new * (1.0 + ep[1:2]) + ep[2:3]


def _mix_kernel(alpha, ya_ref, at_ref, ga0_ref, ga1_ref, gb0_ref, gb1_ref, x_ref, ep_ref,
                wpa_ref, wpb_ref, wout_ref, lng_ref, lnb_ref, xo_ref, ho_ref):
    y_a = jnp.dot(ya_ref[...], wpa_ref[...], preferred_element_type=F32)
    y_b = jnp.dot(at_ref[...], wpb_ref[...], preferred_element_type=F32)
    gate_a = jnp.concatenate([ga0_ref[...], ga1_ref[...]], axis=1).astype(F32)
    gate_b = jnp.concatenate([gb0_ref[...], gb1_ref[...]], axis=1).astype(F32)
    merged = jax.nn.sigmoid(gate_a) * y_a + jax.nn.sigmoid(gate_b) * y_b
    y = jnp.dot(merged.astype(BF16), wout_ref[...], preferred_element_type=F32)
    ep = ep_ref[0]
    x_new = _deepnorm_ln(x_ref[...], y, ep, lng_ref[...], lnb_ref[...], alpha)
    xo_ref[...] = x_new
    ho_ref[...] = _next_h(x_new, ep).astype(ho_ref.dtype)


def _mix(ya, attn, proj, x2d, ep, w_pa, w_pb, w_out, ln_g, ln_b, alpha, seq, ga_col, gb_col, tm=256):
    t, d = x2d.shape
    wa = ya.shape[1]
    half = d // 2
    per_b = seq // tm
    const = lambda i: (0, 0)
    return pl.pallas_call(
        functools.partial(_mix_kernel, alpha),
        grid=(t // tm,),
        in_specs=[
            pl.BlockSpec((tm, wa), lambda i: (i, 0)),
            pl.BlockSpec((tm, wa), lambda i: (i, 0)),
            pl.BlockSpec((tm, half), lambda i: (i, ga_col)),
            pl.BlockSpec((tm, half), lambda i: (i, ga_col + 1)),
            pl.BlockSpec((tm, half), lambda i: (i, gb_col)),
            pl.BlockSpec((tm, half), lambda i: (i, gb_col + 1)),
            pl.BlockSpec((tm, d), lambda i: (i, 0)),
            pl.BlockSpec((1, 3, d), lambda i: (i // per_b, 0, 0)),
            _resident((wa, d), const),
            _resident((wa, d), const),
            _resident((d, d), const),
            _resident((1, d), const),
            _resident((1, d), const),
        ],
        out_specs=[pl.BlockSpec((tm, d), lambda i: (i, 0)), pl.BlockSpec((tm, d), lambda i: (i, 0))],
        out_shape=[jax.ShapeDtypeStruct((t, d), F32), jax.ShapeDtypeStruct((t, d), BF16)],
        compiler_params=_cparams(("arbitrary",), 56),
        name="mix_out_ln",
    )(ya, attn, proj, proj, proj, proj, x2d, ep, w_pa, w_pb, w_out, ln_g.reshape(1, d),
      ln_b.reshape(1, d))


def _swiglu_step(h, wg, wu, wd):
    g = jnp.dot(h, wg, preferred_element_type=F32)
    u = jnp.dot(h, wu, preferred_element_type=F32)
    return jnp.dot((g * jax.nn.sigmoid(g) * u).astype(BF16), wd, preferred_element_type=F32)


def _ffn_kernel(alpha, h_ref, wg_ref, wu_ref, wd_ref, x_ref, ep_ref, lng_ref, lnb_ref,
                xo_ref, ho_ref, acc_ref):
    f = pl.program_id(1)

    @pl.when(f == 0)
    def _():
        acc_ref[...] = jnp.zeros_like(acc_ref)

    acc_ref[...] += _swiglu_step(h_ref[...], wg_ref[...], wu_ref[...], wd_ref[...])

    @pl.when(f == pl.num_programs(1) - 1)
    def _():
        ep = ep_ref[0]
        x_new = _deepnorm_ln(x_ref[...], acc_ref[...], ep, lng_ref[...], lnb_ref[...], alpha)
        xo_ref[...] = x_new
        ho_ref[...] = _next_h(x_new, ep).astype(ho_ref.dtype)


def _ffn(h, x2d, ep, w_gate, w_up, w_down, ln_g, ln_b, alpha, seq, tm=512, tf=512):
    t, d = x2d.shape
    ff = w_gate.shape[1]
    per_b = seq // tm
    return pl.pallas_call(
        functools.partial(_ffn_kernel, alpha),
        grid=(t // tm, ff // tf),
        in_specs=[
            pl.BlockSpec((tm, d), lambda i, f: (i, 0)),
            pl.BlockSpec((d, tf), lambda i, f: (0, f)),
            pl.BlockSpec((d, tf), lambda i, f: (0, f)),
            pl.BlockSpec((tf, d), lambda i, f: (f, 0)),
            pl.BlockSpec((tm, d), lambda i, f: (i, 0)),
            pl.BlockSpec((1, 3, d), lambda i, f: (i // per_b, 0, 0)),
            pl.BlockSpec((1, d), lambda i, f: (0, 0)),
            pl.BlockSpec((1, d), lambda i, f: (0, 0)),
        ],
        out_specs=[pl.BlockSpec((tm, d), lambda i, f: (i, 0)), pl.BlockSpec((tm, d), lambda i, f: (i, 0))],
        out_shape=[jax.ShapeDtypeStruct((t, d), F32), jax.ShapeDtypeStruct((t, d), BF16)],
        scratch_shapes=[pltpu.VMEM((tm, d), F32)],
        compiler_params=_cparams(("arbitrary", "arbitrary"), 56),
        name="ffn_dense_ln",
    )(h, w_gate, w_up, w_down, x2d, ep, ln_g.reshape(1, d), ln_b.reshape(1, d))


def _router_kernel(h_ref, w_ref, b_ref, o_ref):
    logits = jnp.dot(h_ref[...], w_ref[...], preferred_element_type=F32) + b_ref[...]
    lane_i = lax.broadcasted_iota(jnp.int32, logits.shape, 1)
    lane = lane_i.astype(F32)
    lg = jnp.where(lane_i < N_EXPERTS, logits, -jnp.inf)
    m1 = jnp.max(lg, axis=1, keepdims=True)
    i1 = jnp.min(jnp.where(lg == m1, lane, float(LANES)), axis=1, keepdims=True)
    lg2 = jnp.where(lane == i1, -jnp.inf, lg)
    m2 = jnp.max(lg2, axis=1, keepdims=True)
    i2 = jnp.min(jnp.where(lg2 == m2, lane, float(LANES)), axis=1, keepdims=True)
    e2 = jnp.exp(m2 - m1)
    denom = 1.0 + e2
    out = jnp.where(lane_i == 0, i1, 0.0)
    out = jnp.where(lane_i == 1, i2, out)
    out = jnp.where(lane_i == 2, 1.0 / denom, out)
    out = jnp.where(lane_i == 3, e2 / denom, out)
    o_ref[...] = out


def _router(h, w_router_pad, b_router_pad, tm=1024):
    t, d = h.shape
    return pl.pallas_call(
        _router_kernel,
        grid=(t // tm,),
        in_specs=[
            pl.BlockSpec((tm, d), lambda i: (i, 0)),
            pl.BlockSpec((d, LANES), lambda i: (0, 0)),
            pl.BlockSpec((1, LANES), lambda i: (0, 0)),
        ],
        out_specs=pl.BlockSpec((tm, LANES), lambda i: (i, 0)),
        out_shape=jax.ShapeDtypeStruct((t, LANES), F32),
        compiler_params=_cparams(("arbitrary",), 32),
        name="moe_router",
    )(h, w_router_pad, b_router_pad)


def _gmm_kernel(te_ref, tv_ref, xs_ref, wg_ref, wu_ref, wd_ref, rw_ref, o_ref, acc_ref):
    i = pl.program_id(0)
    f = pl.program_id(1)
    last = pl.num_programs(1) - 1
    valid = tv_ref[i] > 0

    @pl.when(jnp.logical_and(valid, f == 0))
    def _():
        acc_ref[...] = jnp.zeros_like(acc_ref)

    @pl.when(valid)
    def _():
        acc_ref[...] += _swiglu_step(xs_ref[...], wg_ref[0], wu_ref[0], wd_ref[0])

    @pl.when(jnp.logical_and(valid, f == last))
    def _():
        o_ref[...] = (acc_ref[...] * rw_ref[...]).astype(o_ref.dtype)

    @pl.when(jnp.logical_and(jnp.logical_not(valid), f == last))
    def _():
        o_ref[...] = jnp.zeros_like(o_ref)


def _gmm(tile_expert, tile_valid, xs, row_w, w_gate, w_up, w_down, tm, tf=512):
    r, d = xs.shape
    ff = w_gate.shape[2]
    nf = ff // tf

    def f_idx(i, f, tv):
        return jnp.where(tv[i] > 0, f, nf - 1)

    grid_spec = pltpu.PrefetchScalarGridSpec(
        num_scalar_prefetch=2,
        grid=(r // tm, nf),
        in_specs=[
            pl.BlockSpec((tm, d), lambda i, f, te, tv: (i, 0)),
            pl.BlockSpec((1, d, tf), lambda i, f, te, tv: (te[i], 0, f_idx(i, f, tv))),
            pl.BlockSpec((1, d, tf), lambda i, f, te, tv: (te[i], 0, f_idx(i, f, tv))),
            pl.BlockSpec((1, tf, d), lambda i, f, te, tv: (te[i], f_idx(i, f, tv), 0)),
            pl.BlockSpec((tm, 1), lambda i, f, te, tv: (i, 0)),
        ],
        out_specs=pl.BlockSpec((tm, d), lambda i, f, te, tv: (i, 0)),
        scratch_shapes=[pltpu.VMEM((tm, d), F32)],
    )
    return pl.pallas_call(
        _gmm_kernel,
        grid_spec=grid_spec,
        out_shape=jax.ShapeDtypeStruct((r, d), BF16),
        compiler_params=_cparams(("arbitrary", "arbitrary"), 56),
        name="moe_grouped_swiglu",
    )(tile_expert, tile_valid, xs, w_gate, w_up, w_down, row_w)


def _ln_kernel(alpha, y_ref, x_ref, ep_ref, lng_ref, lnb_ref, xo_ref, ho_ref):
    ep = ep_ref[0]
    y = y_ref[:, 0, :].astype(F32) + y_ref[:, 1, :].astype(F32)
    x_new = _deepnorm_ln(x_ref[...], y, ep, lng_ref[...], lnb_ref[...], alpha)
    xo_ref[...] = x_new
    ho_ref[...] = _next_h(x_new, ep).astype(ho_ref.dtype)


def _ln(y_pairs, x2d, ep, ln_g, ln_b, alpha, seq, tm=256):
    t, d = x2d.shape
    per_b = seq // tm
    return pl.pallas_call(
        functools.partial(_ln_kernel, alpha),
        grid=(t // tm,),
        in_specs=[
            pl.BlockSpec((tm, 2, d), lambda i: (i, 0, 0)),
            pl.BlockSpec((tm, d), lambda i: (i, 0)),
            pl.BlockSpec((1, 3, d), lambda i: (i // per_b, 0, 0)),
            pl.BlockSpec((1, d), lambda i: (0, 0)),
            pl.BlockSpec((1, d), lambda i: (0, 0)),
        ],
        out_specs=[pl.BlockSpec((tm, d), lambda i: (i, 0)), pl.BlockSpec((tm, d), lambda i: (i, 0))],
        out_shape=[jax.ShapeDtypeStruct((t, d), F32), jax.ShapeDtypeStruct((t, d), BF16)],
        compiler_params=_cparams(("arbitrary",), 48),
        name="moe_combine_ln",
    )(y_pairs, x2d, ep, ln_g.reshape(1, d), ln_b.reshape(1, d))


def _dispatch_plan(route, tm):
    t = route.shape[0]
    n_pairs = 2 * t
    n_rows = n_pairs + N_EXPERTS * tm
    expert = route[:, :2].astype(jnp.int32).reshape(n_pairs)
    weight = route[:, 2:4].reshape(n_pairs)
    onehot = (expert[:, None] == jnp.arange(N_EXPERTS, dtype=jnp.int32)[None, :]).astype(jnp.int32)
    rank = jnp.take_along_axis(jnp.cumsum(onehot, axis=0) - onehot, expert[:, None], axis=1)[:, 0]
    counts = jnp.sum(onehot, axis=0)
    padded = ((counts + tm - 1) // tm) * tm
    ends = jnp.cumsum(padded)
    starts = ends - padded
    pos = starts[expert] + rank
    src_token = jnp.zeros((n_rows,), jnp.int32).at[pos].set(jnp.arange(n_pairs, dtype=jnp.int32) // 2)
    row_w = jnp.zeros((n_rows,), F32).at[pos].set(weight)
    tile_start = jnp.arange(n_rows // tm, dtype=jnp.int32) * tm
    tile_valid = (tile_start < ends[-1]).astype(jnp.int32)
    n_ended = jnp.sum((tile_start[:, None] >= ends[None, :]).astype(jnp.int32), axis=1)
    tile_expert = jnp.minimum(n_ended, N_EXPERTS - 1)
    last_valid_expert = jnp.max(jnp.where(tile_valid > 0, tile_expert, 0))
    tile_expert = jnp.where(tile_valid > 0, tile_expert, last_valid_expert)
    return pos, src_token, row_w.reshape(n_rows, 1), tile_expert, tile_valid


def kernel(x, c, w_in, conv_w, conv_b, w_rgate, b_rgate, w_igate, b_igate, lru_lambda, w_proj_a, w_proj_b, w_out, w_ada, b_ada, ln_mix_g, ln_mix_b, ln_ffn_g, ln_ffn_b, ffn_w_gate, ffn_w_up, ffn_w_down, moe_w_router, moe_b_router, moe_w_gate, moe_w_up, moe_w_down):
    batch, seq, d = x.shape
    depth = w_in.shape[0]
    t = batch * seq
    lru_w = conv_w.shape[2]
    attn_w = w_proj_b.shape[1]
    heads = attn_w // HEAD_DIM
    alpha = (2.0 * depth) ** 0.25
    moe_tm = 1024

    c_pad = jnp.zeros((SUBLANES, d), F32).at[:batch].set(c)
    ada = _ada_all_layers(c_pad, w_ada, b_ada)[:, :batch].reshape(depth, batch, 6, d)
    zeros_bd = jnp.zeros((batch, d), F32)

    def epilogue_params(gate, scale, shift):
        return jnp.stack([gate, scale, shift], axis=1)

    tables = _rope_tables(seq)
    x2d = x.reshape(t, d)
    h = _modulate(x2d, epilogue_params(zeros_bd, ada[0, :, 1], ada[0, :, 0]), seq)

    head_group_w = MOBA_HEADS_PER_STEP * HEAD_DIM
    q_col = 2 * lru_w // head_group_w
    k_col = q_col + attn_w // head_group_w
    v_col = k_col + attn_w // head_group_w
    ga_col = (2 * lru_w + 3 * attn_w) // (d // 2)
    gb_col = ga_col + 2

    for l in range(depth):
        sh1, sc1, g1, sh2, sc2, g2 = (ada[l, :, j] for j in range(6))
        proj = _matmul(h, w_in[l].astype(BF16))
        ya = _lru(proj, conv_w[l], conv_b[l], w_rgate[l], b_rgate[l], w_igate[l], b_igate[l],
                  lru_lambda[l], batch, seq)
        attn = _moba(proj, tables, batch, seq, heads, q_col, k_col, v_col)
        x2d, h = _mix(ya, attn, proj, x2d, epilogue_params(g1, sc2, sh2),
                      w_proj_a[l].astype(BF16), w_proj_b[l].astype(BF16), w_out[l].astype(BF16),
                      ln_mix_g[l], ln_mix_b[l], alpha, seq, ga_col, gb_col)
        if l + 1 < depth:
            ep = epilogue_params(g2, ada[l + 1, :, 1], ada[l + 1, :, 0])
        else:
            ep = epilogue_params(g2, zeros_bd, zeros_bd)
        j = l // 2
        if l % 2 == 0:
            x2d, h = _ffn(h, x2d, ep, ffn_w_gate[j].astype(BF16), ffn_w_up[j].astype(BF16),
                          ffn_w_down[j].astype(BF16), ln_ffn_g[l], ln_ffn_b[l], alpha, seq)
        else:
            w_r_pad = jnp.zeros((d, LANES), BF16).at[:, :N_EXPERTS].set(moe_w_router[j].astype(BF16))
            b_r_pad = jnp.zeros((1, LANES), F32).at[0, :N_EXPERTS].set(moe_b_router[j])
            route = _router(h, w_r_pad, b_r_pad)
            pos, src_token, row_w, tile_expert, tile_valid = _dispatch_plan(route, moe_tm)
            xs = jnp.take(h, src_token, axis=0)
            ys = _gmm(tile_expert, tile_valid, xs, row_w, moe_w_gate[j].astype(BF16),
                      moe_w_up[j].astype(BF16), moe_w_down[j].astype(BF16), moe_tm)
            y_pairs = jnp.take(ys, pos, axis=0).reshape(t, 2, d)
            x2d, h = _ln(y_pairs, x2d, ep, ln_ffn_g[l], ln_ffn_b[l], alpha, seq)
    return x2d.reshape(batch, seq, d)
```

```python
import functools

import jax
import jax.numpy as jnp
from jax import lax
from jax.experimental import pallas as pl
from jax.experimental.pallas import tpu as pltpu

F32 = jnp.float32
BF16 = jnp.bfloat16

LRU_BLOCKS = 8
LRU_BLOCK_W = 128
CONV_WIDTH = 4
LRU_C = 8.0
HEAD_DIM = 128
MOBA_BLOCK = 256
MOBA_TOPK = 3
MOBA_HEADS_PER_STEP = 2
MOBA_SWEEP = 4
ROPE_THETA = 500000.0
ROT_DIM = HEAD_DIM // 4
ROT_HALF = ROT_DIM // 2
NEG_INF = -1e30
N_EXPERTS = 8
LN_EPS = 1e-5
LANES = 128
SUBLANES = 8
MIB = 1024 * 1024


def _cparams(semantics, vmem_mib):
    return pltpu.CompilerParams(dimension_semantics=semantics, vmem_limit_bytes=vmem_mib * MIB)


def _resident(block_shape, index_map):
    return pl.BlockSpec(block_shape, index_map, pipeline_mode=pl.Buffered(1))


def _ada_kernel(c_ref, w_ref, b_ref, o_ref):
    c = c_ref[...]
    c_act = c * jax.nn.sigmoid(c)
    o_ref[0] = jnp.dot(c_act, w_ref[0], preferred_element_type=F32) + b_ref[0]


def _ada_all_layers(c_pad, w_ada, b_ada, tn=1024):
    depth, d, n = w_ada.shape
    rows = c_pad.shape[0]
    return pl.pallas_call(
        _ada_kernel,
        grid=(depth, n // tn),
        in_specs=[
            pl.BlockSpec((rows, d), lambda l, j: (0, 0)),
            pl.BlockSpec((1, d, tn), lambda l, j: (l, 0, j)),
            pl.BlockSpec((1, 1, tn), lambda l, j: (l, 0, j)),
        ],
        out_specs=pl.BlockSpec((1, rows, tn), lambda l, j: (l, 0, j)),
        out_shape=jax.ShapeDtypeStruct((depth, rows, n), F32),
        compiler_params=_cparams(("arbitrary", "arbitrary"), 40),
        name="ada",
    )(c_pad, w_ada, b_ada.reshape(depth, 1, n))


def _modulate_kernel(x_ref, ep_ref, h_ref):
    ep = ep_ref[0]
    h_ref[...] = (x_ref[...] * (1.0 + ep[1:2]) + ep[2:3]).astype(h_ref.dtype)


def _modulate(x2d, ep, seq, tm=512):
    t, d = x2d.shape
    per_b = seq // tm
    return pl.pallas_call(
        _modulate_kernel,
        grid=(t // tm,),
        in_specs=[
            pl.BlockSpec((tm, d), lambda i: (i, 0)),
            pl.BlockSpec((1, 3, d), lambda i: (i // per_b, 0, 0)),
        ],
        out_specs=pl.BlockSpec((tm, d), lambda i: (i, 0)),
        out_shape=jax.ShapeDtypeStruct((t, d), BF16),
        compiler_params=_cparams(("arbitrary",), 32),
        name="modulate",
    )(x2d, ep)


def _matmul_kernel(a_ref, w_ref, o_ref, wb_ref):
    @pl.when(pl.program_id(1) == 0)
    def _():
        wb_ref[...] = w_ref[...].astype(wb_ref.dtype)

    o_ref[...] = jnp.dot(a_ref[...], wb_ref[...], preferred_element_type=F32).astype(o_ref.dtype)


def _matmul(a, w, layer, tm=1024, tn=1024):
    m, k = a.shape
    n = w.shape[2]
    return pl.pallas_call(
        _matmul_kernel,
        grid=(n // tn, m // tm),
        in_specs=[
            pl.BlockSpec((tm, k), lambda j, i: (i, 0)),
            pl.BlockSpec((None, k, tn), lambda j, i: (layer, 0, j)),
        ],
        out_specs=pl.BlockSpec((tm, tn), lambda j, i: (i, j)),
        out_shape=jax.ShapeDtypeStruct((m, n), BF16),
        scratch_shapes=[pltpu.VMEM((k, tn), BF16)],
        compiler_params=_cparams(("arbitrary", "arbitrary"), 48),
        name="in_proj",
    )(a, w)


def _gelu_tanh(x):
    return 0.5 * x * (1.0 + jnp.tanh(0.7978845608028654 * (x + 0.044715 * (x * x * x))))


def _lru_kernel(x_ref, g_ref, cw_ref, cb_ref, wr_ref, br_ref, wi_ref, bi_ref, lam_ref,
                o_ref, tail_ref, carry_ref):
    @pl.when(pl.program_id(1) == 0)
    def _():
        tail_ref[...] = jnp.zeros_like(tail_ref)
        carry_ref[...] = jnp.zeros_like(carry_ref)

    ts, w = x_ref.shape
    groups = ts // SUBLANES
    x3 = x_ref[...].astype(F32).reshape(groups, SUBLANES, w)
    xe = jnp.concatenate([tail_ref[...][None], x3], axis=0)
    sub = lax.broadcasted_iota(jnp.int32, (groups, SUBLANES, w), 1)

    y = cb_ref[...][None] + cw_ref[CONV_WIDTH - 1:CONV_WIDTH, :][None] * x3
    for k in range(1, CONV_WIDTH):
        rolled = pltpu.roll(xe, k, axis=1)
        shifted = jnp.where(sub < k, rolled[:-1], rolled[1:])
        y = y + cw_ref[CONV_WIDTH - 1 - k:CONV_WIDTH - k, :][None] * shifted
    tail_ref[...] = x3[groups - 1]

    xa = y.reshape(ts, w)
    xab = xa.astype(BF16)

    def block_diag(w_ref, b_ref):
        parts = [
            jnp.dot(xab[:, n * LRU_BLOCK_W:(n + 1) * LRU_BLOCK_W], w_ref[n],
                    preferred_element_type=F32)
            for n in range(LRU_BLOCKS)
        ]
        return jnp.concatenate(parts, axis=1) + b_ref[...]

    r = jax.nn.sigmoid(block_diag(wr_ref, br_ref))
    i = jax.nn.sigmoid(block_diag(wi_ref, bi_ref))
    neg_lam = -lam_ref[...]
    softplus = jnp.maximum(neg_lam, 0.0) + jnp.log1p(jnp.exp(-jnp.abs(neg_lam)))
    a = jnp.exp((-LRU_C) * r * softplus)
    u = jnp.sqrt(1.0 - a * a) * (i * xa)

    a3 = a.reshape(groups, SUBLANES, w)
    u3 = u.reshape(groups, SUBLANES, w)
    d = 1
    while d < SUBLANES:
        a_prev = pltpu.roll(a3, d, axis=1)
        u_prev = pltpu.roll(u3, d, axis=1)
        take = sub >= d
        u3 = jnp.where(take, a3 * u_prev + u3, u3)
        a3 = jnp.where(take, a3 * a_prev, a3)
        d *= 2
    carry = carry_ref[...]
    hs = []
    for gi in range(groups):
        hg = u3[gi] + a3[gi] * carry
        hs.append(hg)
        carry = hg[SUBLANES - 1:SUBLANES, :]
    carry_ref[...] = carry
    h = jnp.stack(hs, axis=0).reshape(ts, w)

    o_ref[...] = (h * _gelu_tanh(g_ref[...].astype(F32))).astype(o_ref.dtype)


def _lru(proj, conv_w, conv_b, w_r, b_r, w_i, b_i, lam, batch, seq, ts=256):
    t = proj.shape[0]
    w = conv_w.shape[1]
    per_b = seq // ts
    row = lambda b, s: b * per_b + s
    vec = lambda v: v.reshape(1, w)
    const2 = lambda b, s: (0, 0)
    const3 = lambda b, s: (0, 0, 0)
    return pl.pallas_call(
        _lru_kernel,
        grid=(batch, per_b),
        in_specs=[
            pl.BlockSpec((ts, w), lambda b, s: (row(b, s), 0)),
            pl.BlockSpec((ts, w), lambda b, s: (row(b, s), 1)),
            pl.BlockSpec((CONV_WIDTH, w), const2),
            pl.BlockSpec((1, w), const2),
            pl.BlockSpec((LRU_BLOCKS, LRU_BLOCK_W, LRU_BLOCK_W), const3),
            pl.BlockSpec((1, w), const2),
            pl.BlockSpec((LRU_BLOCKS, LRU_BLOCK_W, LRU_BLOCK_W), const3),
            pl.BlockSpec((1, w), const2),
            pl.BlockSpec((1, w), const2),
        ],
        out_specs=pl.BlockSpec((ts, w), lambda b, s: (row(b, s), 0)),
        out_shape=jax.ShapeDtypeStruct((t, w), BF16),
        scratch_shapes=[pltpu.VMEM((SUBLANES, w), F32), pltpu.VMEM((1, w), F32)],
        compiler_params=_cparams(("arbitrary", "arbitrary"), 48),
        name="conv_rglru",
    )(proj, proj, conv_w, vec(conv_b), w_r.astype(BF16), vec(b_r), w_i.astype(BF16), vec(b_i),
      vec(lam))


def _rotary(x, cos_ref, sin_lo_ref, sin_hi_ref, rows):
    up = pltpu.roll(x, HEAD_DIM - ROT_HALF, axis=1)
    down = pltpu.roll(x, ROT_HALF, axis=1)
    return x * cos_ref[rows, :] + up * sin_lo_ref[rows, :] + down * sin_hi_ref[rows, :]


_NT = (((1,), (1,)), ((), ()))


def _moba_kernel(q_ref, k_ref, v_ref, cos_ref, sin_lo_ref, sin_hi_ref, o_ref,
                 krot_ref, vt_ref, vtg_ref, kmean_ref, sel_ref, raw_ref):
    qi = pl.program_id(2)
    n_heads, n_blk = kmean_ref.shape[0], kmean_ref.shape[1]
    scale2 = HEAD_DIM ** -0.5 * 1.4426950408889634
    head_cols = [slice(hh * HEAD_DIM, (hh + 1) * HEAD_DIM) for hh in range(n_heads)]

    @pl.when(qi == 0)
    def _():
        for hh in range(n_heads):
            for n in range(n_blk):
                rows = pl.ds(n * MOBA_BLOCK, MOBA_BLOCK)
                kr = _rotary(k_ref[rows, head_cols[hh]].astype(F32), cos_ref, sin_lo_ref,
                             sin_hi_ref, rows)
                krot_ref[hh, rows, :] = kr.astype(krot_ref.dtype)
                kmean_ref[hh, n:n + 1, :] = jnp.mean(kr, axis=0, keepdims=True)
                vt = v_ref[rows, head_cols[hh]].astype(F32).T.astype(vt_ref.dtype)
                vt_ref[hh, n] = vt
                c = n % MOBA_SWEEP
                vtg_ref[hh, n // MOBA_SWEEP, :, c * MOBA_BLOCK:(c + 1) * MOBA_BLOCK] = vt

    own_rows = pl.ds(pl.multiple_of(qi * MOBA_BLOCK, MOBA_BLOCK), MOBA_BLOCK)

    qts, init = [], []
    for hh in range(n_heads):
        qt = _rotary(q_ref[:, head_cols[hh]].astype(F32), cos_ref, sin_lo_ref, sin_hi_ref,
                     own_rows).T.astype(BF16)
        qts.append(qt)

        gate = jnp.dot(kmean_ref[hh].astype(BF16), qt, preferred_element_type=F32)
        blk_i = lax.broadcasted_iota(jnp.int32, gate.shape, 0)
        blk = blk_i.astype(F32)
        past = blk_i < qi
        g = jnp.where(past, gate, -jnp.inf)
        sel = jnp.zeros_like(gate)
        for _ in range(MOBA_TOPK):
            best = jnp.max(g, axis=0, keepdims=True)
            idx = jnp.min(jnp.where(g == best, blk, float(n_blk)), axis=0, keepdims=True)
            pick = blk == idx
            sel = jnp.where(pick, 1.0, sel)
            g = jnp.where(pick, -jnp.inf, g)
        sel_ref[hh] = jnp.where(past, sel, 0.0)

        s = jnp.dot(krot_ref[hh, own_rows, :], qt, preferred_element_type=F32) * scale2
        key_id = lax.broadcasted_iota(jnp.int32, s.shape, 0)
        qry_id = lax.broadcasted_iota(jnp.int32, s.shape, 1)
        s = jnp.where(key_id <= qry_id, s, NEG_INF)
        m0 = jnp.max(s, axis=0, keepdims=True)
        p = jnp.exp2(s - m0)
        l0 = jnp.sum(p, axis=0, keepdims=True)
        acc0 = jnp.dot(vt_ref[hh, qi], p.astype(BF16), preferred_element_type=F32)
        init.append((m0, l0, acc0))

    span = MOBA_SWEEP * MOBA_BLOCK
    last_group = n_blk // MOBA_SWEEP - 1

    def raw_scores(hh, j):
        rows = pl.ds(pl.multiple_of(j * span, span), span)
        return jnp.dot(krot_ref[hh, rows, :], qts[hh], preferred_element_type=F32)

    for hh in range(n_heads):
        raw_ref[hh] = raw_scores(hh, 0)

    def past_group(j, carry):
        out = []
        for hh in range(n_heads):
            m, l, acc = carry[hh]
            s = jnp.concatenate([
                jnp.where(sel_ref[hh, pl.ds(j * MOBA_SWEEP + c, 1), :] > 0.0,
                          raw_ref[hh, c * MOBA_BLOCK:(c + 1) * MOBA_BLOCK, :] * scale2, NEG_INF)
                for c in range(MOBA_SWEEP)], axis=0)
            raw_ref[hh] = raw_scores(hh, jnp.minimum(j + 1, last_group))
            m_new = jnp.maximum(m, jnp.max(s, axis=0, keepdims=True))
            alpha = jnp.exp2(m - m_new)
            p = jnp.exp2(s - m_new)
            l = alpha * l + jnp.sum(p, axis=0, keepdims=True)
            acc = alpha * acc + jnp.dot(vtg_ref[hh, j], p.astype(BF16), preferred_element_type=F32)
            out.append((m_new, l, acc))
        return tuple(out)

    n_groups = lax.div(qi + (MOBA_SWEEP - 1), MOBA_SWEEP)
    final = lax.fori_loop(0, n_groups, past_group, tuple(init))
    for hh in range(n_heads):
        _, l, acc = final[hh]
        o_ref[:, head_cols[hh]] = (acc / l).T.astype(o_ref.dtype)


def _rope_tables(seq):
    pos = jnp.arange(seq, dtype=F32)
    inv_freq = ROPE_THETA ** (-jnp.arange(0, ROT_DIM, 2, dtype=F32) / ROT_DIM)
    ang = pos[:, None] * inv_freq[None, :]
    cos, sin = jnp.cos(ang), jnp.sin(ang)
    rest = HEAD_DIM - ROT_DIM
    cos_t = jnp.concatenate([cos, cos, jnp.ones((seq, rest), F32)], axis=1)
    sin_lo = jnp.concatenate([-sin, jnp.zeros((seq, HEAD_DIM - ROT_HALF), F32)], axis=1)
    sin_hi = jnp.concatenate([jnp.zeros((seq, ROT_HALF), F32), sin, jnp.zeros((seq, rest), F32)], axis=1)
    return cos_t, sin_lo, sin_hi


def _moba(proj, tables, batch, seq, heads, q_col, k_col, v_col, hg=MOBA_HEADS_PER_STEP):
    t = proj.shape[0]
    n_blk = seq // MOBA_BLOCK
    gw = hg * HEAD_DIM
    table_spec = _resident((seq, HEAD_DIM), lambda b, h, qi: (0, 0))
    return pl.pallas_call(
        _moba_kernel,
        grid=(batch, heads // hg, n_blk),
        in_specs=[
            pl.BlockSpec((MOBA_BLOCK, gw), lambda b, h, qi: (b * n_blk + qi, q_col + h)),
            pl.BlockSpec((seq, gw), lambda b, h, qi: (b, k_col + h)),
            pl.BlockSpec((seq, gw), lambda b, h, qi: (b, v_col + h)),
            table_spec, table_spec, table_spec,
        ],
        out_specs=pl.BlockSpec((MOBA_BLOCK, gw), lambda b, h, qi: (b * n_blk + qi, h)),
        out_shape=jax.ShapeDtypeStruct((t, heads * HEAD_DIM), BF16),
        scratch_shapes=[
            pltpu.VMEM((hg, seq, HEAD_DIM), BF16),
            pltpu.VMEM((hg, n_blk, HEAD_DIM, MOBA_BLOCK), BF16),
            pltpu.VMEM((hg, n_blk // MOBA_SWEEP, HEAD_DIM, MOBA_SWEEP * MOBA_BLOCK), BF16),
            pltpu.VMEM((hg, n_blk, HEAD_DIM), F32),
            pltpu.VMEM((hg, n_blk, MOBA_BLOCK), F32),
            pltpu.VMEM((hg, MOBA_SWEEP * MOBA_BLOCK, MOBA_BLOCK), F32),
        ],
        compiler_params=_cparams(("arbitrary", "arbitrary", "arbitrary"), 48),
        name="moba_attention",
    )(proj, proj, proj, *tables)


def _deepnorm_ln(x, y, ep, ln_g, ln_b, alpha):
    r = alpha * x + (1.0 + ep[0:1]) * y
    mu = jnp.mean(r, axis=-1, keepdims=True)
    rc = r - mu
    var = jnp.mean(rc * rc, axis=-1, keepdims=True)
    return rc * lax.rsqrt(var + LN_EPS) * ln_g + ln_b


def _next_h(x_new, ep):
    return x_new * (1.0 + ep[1:2]) + ep[2:3]


def _mix_kernel(alpha, ya_ref, at_ref, ga0_ref, ga1_ref, gb0_ref, gb1_ref, x_ref, ep_ref,
                wpa_ref, wpb_ref, wout_ref, lng_ref, lnb_ref, xo_ref, ho_ref):
    y_a = jnp.dot(ya_ref[...], wpa_ref[...], preferred_element_type=F32)
    y_b = jnp.dot(at_ref[...], wpb_ref[...], preferred_element_type=F32)
    gate_a = jnp.concatenate([ga0_ref[...], ga1_ref[...]], axis=1).astype(F32)
    gate_b = jnp.concatenate([gb0_ref[...], gb1_ref[...]], axis=1).astype(F32)
    merged = jax.nn.sigmoid(gate_a) * y_a + jax.nn.sigmoid(gate_b) * y_b
    y = jnp.dot(merged.astype(BF16), wout_ref[...], preferred_element_type=F32)
    ep = ep_ref[0]
    x_new = _deepnorm_ln(x_ref[...], y, ep, lng_ref[...], lnb_ref[...], alpha)
    xo_ref[...] = x_new
    ho_ref[...] = _next_h(x_new, ep).astype(ho_ref.dtype)


def _mix(ya, attn, proj, x2d, ep, w_pa, w_pb, w_out, ln_g, ln_b, alpha, seq, ga_col, gb_col, tm=256):
    t, d = x2d.shape
    wa = ya.shape[1]
    half = d // 2
    per_b = seq // tm
    const = lambda i: (0, 0)
    return pl.pallas_call(
        functools.partial(_mix_kernel, alpha),
        grid=(t // tm,),
        in_specs=[
            pl.BlockSpec((tm, wa), lambda i: (i, 0)),
            pl.BlockSpec((tm, wa), lambda i: (i, 0)),
            pl.BlockSpec((tm, half), lambda i: (i, ga_col)),
            pl.BlockSpec((tm, half), lambda i: (i, ga_col + 1)),
            pl.BlockSpec((tm, half), lambda i: (i, gb_col)),
            pl.BlockSpec((tm, half), lambda i: (i, gb_col + 1)),
            pl.BlockSpec((tm, d), lambda i: (i, 0)),
            pl.BlockSpec((1, 3, d), lambda i: (i // per_b, 0, 0)),
            _resident((wa, d), const),
            _resident((wa, d), const),
            _resident((d, d), const),
            _resident((1, d), const),
            _resident((1, d), const),
        ],
        out_specs=[pl.BlockSpec((tm, d), lambda i: (i, 0)), pl.BlockSpec((tm, d), lambda i: (i, 0))],
        out_shape=[jax.ShapeDtypeStruct((t, d), F32), jax.ShapeDtypeStruct((t, d), BF16)],
        compiler_params=_cparams(("arbitrary",), 56),
        name="mix_out_ln",
    )(ya, attn, proj, proj, proj, proj, x2d, ep, w_pa, w_pb, w_out, ln_g.reshape(1, d),
      ln_b.reshape(1, d))


def _swiglu_step(h, wg, wu, wd):
    g = jnp.dot(h, wg, preferred_element_type=F32)
    u = jnp.dot(h, wu, preferred_element_type=F32)
    return jnp.dot((g * jax.nn.sigmoid(g) * u).astype(BF16), wd, preferred_element_type=F32)


def _ffn_kernel(alpha, h_ref, wg_ref, wu_ref, wd_ref, x_ref, ep_ref, lng_ref, lnb_ref,
                xo_ref, ho_ref, acc_ref):
    f = pl.program_id(1)

    @pl.when(f == 0)
    def _():
        acc_ref[...] = jnp.zeros_like(acc_ref)

    acc_ref[...] += _swiglu_step(h_ref[...], wg_ref[...], wu_ref[...], wd_ref[...])

    @pl.when(f == pl.num_programs(1) - 1)
    def _():
        ep = ep_ref[0]
        x_new = _deepnorm_ln(x_ref[...], acc_ref[...], ep, lng_ref[...], lnb_ref[...], alpha)
        xo_ref[...] = x_new
        ho_ref[...] = _next_h(x_new, ep).astype(ho_ref.dtype)


def _ffn(h, x2d, ep, w_gate, w_up, w_down, ln_g, ln_b, alpha, seq, tm=512, tf=512):
    t, d = x2d.shape
    ff = w_gate.shape[1]
    per_b = seq // tm
    return pl.pallas_call(
        functools.partial(_ffn_kernel, alpha),
        grid=(t // tm, ff // tf),
        in_specs=[
            pl.BlockSpec((tm, d), lambda i, f: (i, 0)),
            pl.BlockSpec((d, tf), lambda i, f: (0, f)),
            pl.BlockSpec((d, tf), lambda i, f: (0, f)),
            pl.BlockSpec((tf, d), lambda i, f: (f, 0)),
            pl.BlockSpec((tm, d), lambda i, f: (i, 0)),
            pl.BlockSpec((1, 3, d), lambda i, f: (i // per_b, 0, 0)),
            pl.BlockSpec((1, d), lambda i, f: (0, 0)),
            pl.BlockSpec((1, d), lambda i, f: (0, 0)),
        ],
        out_specs=[pl.BlockSpec((tm, d), lambda i, f: (i, 0)), pl.BlockSpec((tm, d), lambda i, f: (i, 0))],
        out_shape=[jax.ShapeDtypeStruct((t, d), F32), jax.ShapeDtypeStruct((t, d), BF16)],
        scratch_shapes=[pltpu.VMEM((tm, d), F32)],
        compiler_params=_cparams(("arbitrary", "arbitrary"), 56),
        name="ffn_dense_ln",
    )(h, w_gate, w_up, w_down, x2d, ep, ln_g.reshape(1, d), ln_b.reshape(1, d))


def _router_kernel(h_ref, w_ref, b_ref, o_ref):
    logits = jnp.dot(h_ref[...], w_ref[...], preferred_element_type=F32) + b_ref[...]
    lane_i = lax.broadcasted_iota(jnp.int32, logits.shape, 1)
    lane = lane_i.astype(F32)
    lg = jnp.where(lane_i < N_EXPERTS, logits, -jnp.inf)
    m1 = jnp.max(lg, axis=1, keepdims=True)
    i1 = jnp.min(jnp.where(lg == m1, lane, float(LANES)), axis=1, keepdims=True)
    lg2 = jnp.where(lane == i1, -jnp.inf, lg)
    m2 = jnp.max(lg2, axis=1, keepdims=True)
    i2 = jnp.min(jnp.where(lg2 == m2, lane, float(LANES)), axis=1, keepdims=True)
    e2 = jnp.exp(m2 - m1)
    denom = 1.0 + e2
    out = jnp.where(lane_i == 0, i1, 0.0)
    out = jnp.where(lane_i == 1, i2, out)
    out = jnp.where(lane_i == 2, 1.0 / denom, out)
    out = jnp.where(lane_i == 3, e2 / denom, out)
    o_ref[...] = out


def _router(h, w_router_pad, b_router_pad, tm=1024):
    t, d = h.shape
    return pl.pallas_call(
        _router_kernel,
        grid=(t // tm,),
        in_specs=[
            pl.BlockSpec((tm, d), lambda i: (i, 0)),
            pl.BlockSpec((d, LANES), lambda i: (0, 0)),
            pl.BlockSpec((1, LANES), lambda i: (0, 0)),
        ],
        out_specs=pl.BlockSpec((tm, LANES), lambda i: (i, 0)),
        out_shape=jax.ShapeDtypeStruct((t, LANES), F32),
        compiler_params=_cparams(("arbitrary",), 32),
        name="moe_router",
    )(h, w_router_pad, b_router_pad)


def _gmm_kernel(te_ref, tv_ref, xs_ref, wg_ref, wu_ref, wd_ref, rw_ref, o_ref, acc_ref):
    i = pl.program_id(0)
    f = pl.program_id(1)
    last = pl.num_programs(1) - 1
    valid = tv_ref[i] > 0

    @pl.when(jnp.logical_and(valid, f == 0))
    def _():
        acc_ref[...] = jnp.zeros_like(acc_ref)

    @pl.when(valid)
    def _():
        acc_ref[...] += _swiglu_step(xs_ref[...], wg_ref[...].astype(BF16),
                                     wu_ref[...].astype(BF16), wd_ref[...].astype(BF16))

    @pl.when(jnp.logical_and(valid, f == last))
    def _():
        o_ref[...] = (acc_ref[...] * rw_ref[...]).astype(o_ref.dtype)

    @pl.when(jnp.logical_and(jnp.logical_not(valid), f == last))
    def _():
        o_ref[...] = jnp.zeros_like(o_ref)


def _gmm(tile_expert, tile_valid, xs, row_w, w_gate, w_up, w_down, layer, tm, tf=256):
    r, d = xs.shape
    ff = w_gate.shape[3]
    nf = ff // tf

    def f_idx(i, f, tv):
        return jnp.where(tv[i] > 0, f, nf - 1)

    grid_spec = pltpu.PrefetchScalarGridSpec(
        num_scalar_prefetch=2,
        grid=(r // tm, nf),
        in_specs=[
            pl.BlockSpec((tm, d), lambda i, f, te, tv: (i, 0)),
            pl.BlockSpec((None, None, d, tf), lambda i, f, te, tv: (layer, te[i], 0, f_idx(i, f, tv))),
            pl.BlockSpec((None, None, d, tf), lambda i, f, te, tv: (layer, te[i], 0, f_idx(i, f, tv))),
            pl.BlockSpec((None, None, tf, d), lambda i, f, te, tv: (layer, te[i], f_idx(i, f, tv), 0)),
            pl.BlockSpec((tm, 1), lambda i, f, te, tv: (i, 0)),
        ],
        out_specs=pl.BlockSpec((tm, d), lambda i, f, te, tv: (i, 0)),
        scratch_shapes=[pltpu.VMEM((tm, d), F32)],
    )
    return pl.pallas_call(
        _gmm_kernel,
        grid_spec=grid_spec,
        out_shape=jax.ShapeDtypeStruct((r, d), BF16),
        compiler_params=_cparams(("arbitrary", "arbitrary"), 56),
        name="moe_grouped_swiglu",
    )(tile_expert, tile_valid, xs, w_gate, w_up, w_down, row_w)


def _ln_kernel(alpha, y_ref, x_ref, ep_ref, lng_ref, lnb_ref, xo_ref, ho_ref):
    ep = ep_ref[0]
    y = y_ref[:, 0, :].astype(F32) + y_ref[:, 1, :].astype(F32)
    x_new = _deepnorm_ln(x_ref[...], y, ep, lng_ref[...], lnb_ref[...], alpha)
    xo_ref[...] = x_new
    ho_ref[...] = _next_h(x_new, ep).astype(ho_ref.dtype)


def _ln(y_pairs, x2d, ep, ln_g, ln_b, alpha, seq, tm=256):
    t, d = x2d.shape
    per_b = seq // tm
    return pl.pallas_call(
        functools.partial(_ln_kernel, alpha),
        grid=(t // tm,),
        in_specs=[
            pl.BlockSpec((tm, 2, d), lambda i: (i, 0, 0)),
            pl.BlockSpec((tm, d), lambda i: (i, 0)),
            pl.BlockSpec((1, 3, d), lambda i: (i // per_b, 0, 0)),
            pl.BlockSpec((1, d), lambda i: (0, 0)),
            pl.BlockSpec((1, d), lambda i: (0, 0)),
        ],
        out_specs=[pl.BlockSpec((tm, d), lambda i: (i, 0)), pl.BlockSpec((tm, d), lambda i: (i, 0))],
        out_shape=[jax.ShapeDtypeStruct((t, d), F32), jax.ShapeDtypeStruct((t, d), BF16)],
        compiler_params=_cparams(("arbitrary",), 48),
        name="moe_combine_ln",
    )(y_pairs, x2d, ep, ln_g.reshape(1, d), ln_b.reshape(1, d))


def _dispatch_plan(route, tm):
    t = route.shape[0]
    n_pairs = 2 * t
    n_rows = n_pairs + N_EXPERTS * tm
    expert = route[:, :2].astype(jnp.int32).reshape(n_pairs)
    weight = route[:, 2:4].reshape(n_pairs)
    onehot = (expert[:, None] == jnp.arange(N_EXPERTS, dtype=jnp.int32)[None, :]).astype(jnp.int32)
    rank = jnp.take_along_axis(jnp.cumsum(onehot, axis=0) - onehot, expert[:, None], axis=1)[:, 0]
    counts = jnp.sum(onehot, axis=0)
    padded = ((counts + tm - 1) // tm) * tm
    ends = jnp.cumsum(padded)
    starts = ends - padded
    pos = starts[expert] + rank
    src_token = jnp.zeros((n_rows,), jnp.int32).at[pos].set(jnp.arange(n_pairs, dtype=jnp.int32) // 2)
    row_w = jnp.zeros((n_rows,), F32).at[pos].set(weight)
    tile_start = jnp.arange(n_rows // tm, dtype=jnp.int32) * tm
    tile_valid = (tile_start < ends[-1]).astype(jnp.int32)
    n_ended = jnp.sum((tile_start[:, None] >= ends[None, :]).astype(jnp.int32), axis=1)
    tile_expert = jnp.minimum(n_ended, N_EXPERTS - 1)
    last_valid_expert = jnp.max(jnp.where(tile_valid > 0, tile_expert, 0))
    tile_expert = jnp.where(tile_valid > 0, tile_expert, last_valid_expert)
    return pos, src_token, row_w.reshape(n_rows, 1), tile_expert, tile_valid


def kernel(x, c, w_in, conv_w, conv_b, w_rgate, b_rgate, w_igate, b_igate, lru_lambda, w_proj_a, w_proj_b, w_out, w_ada, b_ada, ln_mix_g, ln_mix_b, ln_ffn_g, ln_ffn_b, ffn_w_gate, ffn_w_up, ffn_w_down, moe_w_router, moe_b_router, moe_w_gate, moe_w_up, moe_w_down):
    batch, seq, d = x.shape
    depth = w_in.shape[0]
    t = batch * seq
    lru_w = conv_w.shape[2]
    attn_w = w_proj_b.shape[1]
    heads = attn_w // HEAD_DIM
    alpha = (2.0 * depth) ** 0.25
    moe_tm = 1024

    c_pad = jnp.zeros((SUBLANES, d), F32).at[:batch].set(c)
    ada = _ada_all_layers(c_pad, w_ada, b_ada)[:, :batch].reshape(depth, batch, 6, d)
    zeros_bd = jnp.zeros((batch, d), F32)

    def epilogue_params(gate, scale, shift):
        return jnp.stack([gate, scale, shift], axis=1)

    tables = _rope_tables(seq)
    x2d = x.reshape(t, d)
    h = _modulate(x2d, epilogue_params(zeros_bd, ada[0, :, 1], ada[0, :, 0]), seq)

    head_group_w = MOBA_HEADS_PER_STEP * HEAD_DIM
    q_col = 2 * lru_w // head_group_w
    k_col = q_col + attn_w // head_group_w
    v_col = k_col + attn_w // head_group_w
    ga_col = (2 * lru_w + 3 * attn_w) // (d // 2)
    gb_col = ga_col + 2

    for l in range(depth):
        sh1, sc1, g1, sh2, sc2, g2 = (ada[l, :, j] for j in range(6))
        proj = _matmul(h, w_in, l)
        ya = _lru(proj, conv_w[l], conv_b[l], w_rgate[l], b_rgate[l], w_igate[l], b_igate[l],
                  lru_lambda[l], batch, seq)
        attn = _moba(proj, tables, batch, seq, heads, q_col, k_col, v_col)
        x2d, h = _mix(ya, attn, proj, x2d, epilogue_params(g1, sc2, sh2),
                      w_proj_a[l].astype(BF16), w_proj_b[l].astype(BF16), w_out[l].astype(BF16),
                      ln_mix_g[l], ln_mix_b[l], alpha, seq, ga_col, gb_col)
        if l + 1 < depth:
            ep = epilogue_params(g2, ada[l + 1, :, 1], ada[l + 1, :, 0])
        else:
            ep = epilogue_params(g2, zeros_bd, zeros_bd)
        j = l // 2
        if l % 2 == 0:
            x2d, h = _ffn(h, x2d, ep, ffn_w_gate[j].astype(BF16), ffn_w_up[j].astype(BF16),
                          ffn_w_down[j].astype(BF16), ln_ffn_g[l], ln_ffn_b[l], alpha, seq)
        else:
            w_r_pad = jnp.zeros((d, LANES), BF16).at[:, :N_EXPERTS].set(moe_w_router[j].astype(BF16))
            b_r_pad = jnp.zeros((1, LANES), F32).at[0, :N_EXPERTS].set(moe_b_router[j])
            route = _router(h, w_r_pad, b_r_pad)
            pos, src_token, row_w, tile_expert, tile_valid = _dispatch_plan(route, moe_tm)
            xs = jnp.take(h, src_token, axis=0)
            ys = _gmm(tile_expert, tile_valid, xs, row_w, moe_w_gate, moe_w_up, moe_w_down, j,
                      moe_tm)
            y_pairs = jnp.take(ys, pos, axis=0).reshape(t, 2, d)
            x2d, h = _ln(y_pairs, x2d, ep, ln_ffn_g[l], ln_ffn_b[l], alpha, seq)
    return x2d.reshape(batch, seq, d)
```

```python
import functools

import jax
import jax.numpy as jnp
from jax import lax
from jax.experimental import pallas as pl
from jax.experimental.pallas import tpu as pltpu

F32 = jnp.float32
BF16 = jnp.bfloat16

LRU_BLOCKS = 8
LRU_BLOCK_W = 128
CONV_WIDTH = 4
LRU_C = 8.0
HEAD_DIM = 128
MOBA_BLOCK = 256
MOBA_TOPK = 3
MOBA_HEADS_PER_STEP = 2
MOBA_SWEEP = 4
ROPE_THETA = 500000.0
ROT_DIM = HEAD_DIM // 4
ROT_HALF = ROT_DIM // 2
NEG_INF = -1e30
N_EXPERTS = 8
LN_EPS = 1e-5
LANES = 128
SUBLANES = 8
MIB = 1024 * 1024


def _cparams(semantics, vmem_mib):
    return pltpu.CompilerParams(dimension_semantics=semantics, vmem_limit_bytes=vmem_mib * MIB)


def _resident(block_shape, index_map):
    return pl.BlockSpec(block_shape, index_map, pipeline_mode=pl.Buffered(1))


def _ada_kernel(c_ref, w_ref, b_ref, o_ref):
    c = c_ref[...]
    c_act = c * jax.nn.sigmoid(c)
    o_ref[0] = jnp.dot(c_act, w_ref[0], preferred_element_type=F32) + b_ref[0]


def _ada_all_layers(c_pad, w_ada, b_ada, tn=1024):
    depth, d, n = w_ada.shape
    rows = c_pad.shape[0]
    return pl.pallas_call(
        _ada_kernel,
        grid=(depth, n // tn),
        in_specs=[
            pl.BlockSpec((rows, d), lambda l, j: (0, 0)),
            pl.BlockSpec((1, d, tn), lambda l, j: (l, 0, j)),
            pl.BlockSpec((1, 1, tn), lambda l, j: (l, 0, j)),
        ],
        out_specs=pl.BlockSpec((1, rows, tn), lambda l, j: (l, 0, j)),
        out_shape=jax.ShapeDtypeStruct((depth, rows, n), F32),
        compiler_params=_cparams(("arbitrary", "arbitrary"), 40),
        name="ada",
    )(c_pad, w_ada, b_ada.reshape(depth, 1, n))


def _modulate_kernel(x_ref, ep_ref, h_ref):
    ep = ep_ref[0]
    h_ref[...] = (x_ref[...] * (1.0 + ep[1:2]) + ep[2:3]).astype(h_ref.dtype)


def _modulate(x2d, ep, seq, tm=512):
    t, d = x2d.shape
    per_b = seq // tm
    return pl.pallas_call(
        _modulate_kernel,
        grid=(t // tm,),
        in_specs=[
            pl.BlockSpec((tm, d), lambda i: (i, 0)),
            pl.BlockSpec((1, 3, d), lambda i: (i // per_b, 0, 0)),
        ],
        out_specs=pl.BlockSpec((tm, d), lambda i: (i, 0)),
        out_shape=jax.ShapeDtypeStruct((t, d), BF16),
        compiler_params=_cparams(("arbitrary",), 32),
        name="modulate",
    )(x2d, ep)


def _matmul_kernel(a_ref, w_ref, o_ref, wb_ref):
    @pl.when(pl.program_id(1) == 0)
    def _():
        wb_ref[...] = w_ref[...].astype(wb_ref.dtype)

    o_ref[...] = jnp.dot(a_ref[...], wb_ref[...], preferred_element_type=F32).astype(o_ref.dtype)


def _matmul(a, w, layer, tm=1024, tn=1024):
    m, k = a.shape
    n = w.shape[2]
    return pl.pallas_call(
        _matmul_kernel,
        grid=(n // tn, m // tm),
        in_specs=[
            pl.BlockSpec((tm, k), lambda j, i: (i, 0)),
            pl.BlockSpec((None, k, tn), lambda j, i: (layer, 0, j)),
        ],
        out_specs=pl.BlockSpec((tm, tn), lambda j, i: (i, j)),
        out_shape=jax.ShapeDtypeStruct((m, n), BF16),
        scratch_shapes=[pltpu.VMEM((k, tn), BF16)],
        compiler_params=_cparams(("arbitrary", "arbitrary"), 48),
        name="in_proj",
    )(a, w)


def _gelu_tanh(x):
    return 0.5 * x * (1.0 + jnp.tanh(0.7978845608028654 * (x + 0.044715 * (x * x * x))))


def _lru_kernel(x_ref, g_ref, cw_ref, cb_ref, wr_ref, br_ref, wi_ref, bi_ref, lam_ref,
                o_ref, tail_ref, carry_ref):
    @pl.when(pl.program_id(1) == 0)
    def _():
        tail_ref[...] = jnp.zeros_like(tail_ref)
        carry_ref[...] = jnp.zeros_like(carry_ref)

    ts, w = x_ref.shape
    groups = ts // SUBLANES
    x3 = x_ref[...].astype(F32).reshape(groups, SUBLANES, w)
    xe = jnp.concatenate([tail_ref[...][None], x3], axis=0)
    sub = lax.broadcasted_iota(jnp.int32, (groups, SUBLANES, w), 1)

    y = cb_ref[...][None] + cw_ref[CONV_WIDTH - 1:CONV_WIDTH, :][None] * x3
    for k in range(1, CONV_WIDTH):
        rolled = pltpu.roll(xe, k, axis=1)
        shifted = jnp.where(sub < k, rolled[:-1], rolled[1:])
        y = y + cw_ref[CONV_WIDTH - 1 - k:CONV_WIDTH - k, :][None] * shifted
    tail_ref[...] = x3[groups - 1]

    xa = y.reshape(ts, w)
    xab = xa.astype(BF16)

    def block_diag(w_ref, b_ref):
        parts = [
            jnp.dot(xab[:, n * LRU_BLOCK_W:(n + 1) * LRU_BLOCK_W], w_ref[n],
                    preferred_element_type=F32)
            for n in range(LRU_BLOCKS)
        ]
        return jnp.concatenate(parts, axis=1) + b_ref[...]

    r = jax.nn.sigmoid(block_diag(wr_ref, br_ref))
    i = jax.nn.sigmoid(block_diag(wi_ref, bi_ref))
    neg_lam = -lam_ref[...]
    softplus = jnp.maximum(neg_lam, 0.0) + jnp.log1p(jnp.exp(-jnp.abs(neg_lam)))
    a = jnp.exp((-LRU_C) * r * softplus)
    u = jnp.sqrt(1.0 - a * a) * (i * xa)

    a3 = a.reshape(groups, SUBLANES, w)
    u3 = u.reshape(groups, SUBLANES, w)
    d = 1
    while d < SUBLANES:
        a_prev = pltpu.roll(a3, d, axis=1)
        u_prev = pltpu.roll(u3, d, axis=1)
        take = sub >= d
        u3 = jnp.where(take, a3 * u_prev + u3, u3)
        a3 = jnp.where(take, a3 * a_prev, a3)
        d *= 2
    carry = carry_ref[...]
    hs = []
    for gi in range(groups):
        hg = u3[gi] + a3[gi] * carry
        hs.append(hg)
        carry = hg[SUBLANES - 1:SUBLANES, :]
    carry_ref[...] = carry
    h = jnp.stack(hs, axis=0).reshape(ts, w)

    o_ref[...] = (h * _gelu_tanh(g_ref[...].astype(F32))).astype(o_ref.dtype)


def _lru(proj, conv_w, conv_b, w_r, b_r, w_i, b_i, lam, batch, seq, ts=256):
    t = proj.shape[0]
    w = conv_w.shape[1]
    per_b = seq // ts
    row = lambda b, s: b * per_b + s
    vec = lambda v: v.reshape(1, w)
    const2 = lambda b, s: (0, 0)
    const3 = lambda b, s: (0, 0, 0)
    return pl.pallas_call(
        _lru_kernel,
        grid=(batch, per_b),
        in_specs=[
            pl.BlockSpec((ts, w), lambda b, s: (row(b, s), 0)),
            pl.BlockSpec((ts, w), lambda b, s: (row(b, s), 1)),
            pl.BlockSpec((CONV_WIDTH, w), const2),
            pl.BlockSpec((1, w), const2),
            pl.BlockSpec((LRU_BLOCKS, LRU_BLOCK_W, LRU_BLOCK_W), const3),
            pl.BlockSpec((1, w), const2),
            pl.BlockSpec((LRU_BLOCKS, LRU_BLOCK_W, LRU_BLOCK_W), const3),
            pl.BlockSpec((1, w), const2),
            pl.BlockSpec((1, w), const2),
        ],
        out_specs=pl.BlockSpec((ts, w), lambda b, s: (row(b, s), 0)),
        out_shape=jax.ShapeDtypeStruct((t, w), BF16),
        scratch_shapes=[pltpu.VMEM((SUBLANES, w), F32), pltpu.VMEM((1, w), F32)],
        compiler_params=_cparams(("arbitrary", "arbitrary"), 48),
        name="conv_rglru",
    )(proj, proj, conv_w, vec(conv_b), w_r.astype(BF16), vec(b_r), w_i.astype(BF16), vec(b_i),
      vec(lam))


def _rotary(x, cos_ref, sin_lo_ref, sin_hi_ref, rows):
    up = pltpu.roll(x, HEAD_DIM - ROT_HALF, axis=1)
    down = pltpu.roll(x, ROT_HALF, axis=1)
    return x * cos_ref[rows, :] + up * sin_lo_ref[rows, :] + down * sin_hi_ref[rows, :]


_NT = (((1,), (1,)), ((), ()))


def _moba_kernel(q_ref, k_ref, v_ref, cos_ref, sin_lo_ref, sin_hi_ref, o_ref,
                 krot_ref, vt_ref, vtg_ref, kmean_ref, sel_ref, raw_ref):
    qi = pl.program_id(2)
    n_heads, n_blk = kmean_ref.shape[0], kmean_ref.shape[1]
    scale2 = HEAD_DIM ** -0.5 * 1.4426950408889634
    head_cols = [slice(hh * HEAD_DIM, (hh + 1) * HEAD_DIM) for hh in range(n_heads)]

    @pl.when(qi == 0)
    def _():
        for hh in range(n_heads):
            for n in range(n_blk):
                rows = pl.ds(n * MOBA_BLOCK, MOBA_BLOCK)
                kr = _rotary(k_ref[rows, head_cols[hh]].astype(F32), cos_ref, sin_lo_ref,
                             sin_hi_ref, rows)
                krot_ref[hh, rows, :] = kr.astype(krot_ref.dtype)
                kmean_ref[hh, n:n + 1, :] = jnp.mean(kr, axis=0, keepdims=True)
                vt = v_ref[rows, head_cols[hh]].astype(F32).T.astype(vt_ref.dtype)
                vt_ref[hh, n] = vt
                c = n % MOBA_SWEEP
                vtg_ref[hh, n // MOBA_SWEEP, :, c * MOBA_BLOCK:(c + 1) * MOBA_BLOCK] = vt

    own_rows = pl.ds(pl.multiple_of(qi * MOBA_BLOCK, MOBA_BLOCK), MOBA_BLOCK)

    qts, init = [], []
    for hh in range(n_heads):
        qt = _rotary(q_ref[:, head_cols[hh]].astype(F32), cos_ref, sin_lo_ref, sin_hi_ref,
                     own_rows).T.astype(BF16)
        qts.append(qt)

        gate = jnp.dot(kmean_ref[hh].astype(BF16), qt, preferred_element_type=F32)
        blk_i = lax.broadcasted_iota(jnp.int32, gate.shape, 0)
        blk = blk_i.astype(F32)
        past = blk_i < qi
        g = jnp.where(past, gate, -jnp.inf)
        sel = jnp.zeros_like(gate)
        for _ in range(MOBA_TOPK):
            best = jnp.max(g, axis=0, keepdims=True)
            idx = jnp.min(jnp.where(g == best, blk, float(n_blk)), axis=0, keepdims=True)
            pick = blk == idx
            sel = jnp.where(pick, 1.0, sel)
            g = jnp.where(pick, -jnp.inf, g)
        sel_ref[hh] = jnp.where(past, sel, 0.0)

        s = jnp.dot(krot_ref[hh, own_rows, :], qt, preferred_element_type=F32) * scale2
        key_id = lax.broadcasted_iota(jnp.int32, s.shape, 0)
        qry_id = lax.broadcasted_iota(jnp.int32, s.shape, 1)
        s = jnp.where(key_id <= qry_id, s, NEG_INF)
        m0 = jnp.max(s, axis=0, keepdims=True)
        p = jnp.exp2(s - m0)
        l0 = jnp.sum(p, axis=0, keepdims=True)
        acc0 = jnp.dot(vt_ref[hh, qi], p.astype(BF16), preferred_element_type=F32)
        init.append((m0, l0, acc0))

    span = MOBA_SWEEP * MOBA_BLOCK
    last_group = n_blk // MOBA_SWEEP - 1

    def raw_scores(hh, j):
        rows = pl.ds(pl.multiple_of(j * span, span), span)
        return jnp.dot(krot_ref[hh, rows, :], qts[hh], preferred_element_type=F32)

    for hh in range(n_heads):
        raw_ref[hh] = raw_scores(hh, 0)

    def past_group(j, carry):
        out = []
        for hh in range(n_heads):
            m, l, acc = carry[hh]
            s = jnp.concatenate([
                jnp.where(sel_ref[hh, pl.ds(j * MOBA_SWEEP + c, 1), :] > 0.0,
                          raw_ref[hh, c * MOBA_BLOCK:(c + 1) * MOBA_BLOCK, :] * scale2, NEG_INF)
                for c in range(MOBA_SWEEP)], axis=0)
            raw_ref[hh] = raw_scores(hh, jnp.minimum(j + 1, last_group))
            m_new = jnp.maximum(m, jnp.max(s, axis=0, keepdims=True))
            alpha = jnp.exp2(m - m_new)
            p = jnp.exp2(s - m_new)
            l = alpha * l + jnp.sum(p, axis=0, keepdims=True)
            acc = alpha * acc + jnp.dot(vtg_ref[hh, j], p.astype(BF16), preferred_element_type=F32)
            out.append((m_new, l, acc))
        return tuple(out)

    n_groups = lax.div(qi + (MOBA_SWEEP - 1), MOBA_SWEEP)
    final = lax.fori_loop(0, n_groups, past_group, tuple(init))
    for hh in range(n_heads):
        _, l, acc = final[hh]
        o_ref[:, head_cols[hh]] = (acc / l).T.astype(o_ref.dtype)


def _rope_tables(seq):
    pos = jnp.arange(seq, dtype=F32)
    inv_freq = ROPE_THETA ** (-jnp.arange(0, ROT_DIM, 2, dtype=F32) / ROT_DIM)
    ang = pos[:, None] * inv_freq[None, :]
    cos, sin = jnp.cos(ang), jnp.sin(ang)
    rest = HEAD_DIM - ROT_DIM
    cos_t = jnp.concatenate([cos, cos, jnp.ones((seq, rest), F32)], axis=1)
    sin_lo = jnp.concatenate([-sin, jnp.zeros((seq, HEAD_DIM - ROT_HALF), F32)], axis=1)
    sin_hi = jnp.concatenate([jnp.zeros((seq, ROT_HALF), F32), sin, jnp.zeros((seq, rest), F32)], axis=1)
    return cos_t, sin_lo, sin_hi


def _moba(proj, tables, batch, seq, heads, q_col, k_col, v_col, hg=MOBA_HEADS_PER_STEP):
    t = proj.shape[0]
    n_blk = seq // MOBA_BLOCK
    gw = hg * HEAD_DIM
    table_spec = _resident((seq, HEAD_DIM), lambda b, h, qi: (0, 0))
    return pl.pallas_call(
        _moba_kernel,
        grid=(batch, heads // hg, n_blk),
        in_specs=[
            pl.BlockSpec((MOBA_BLOCK, gw), lambda b, h, qi: (b * n_blk + qi, q_col + h)),
            pl.BlockSpec((seq, gw), lambda b, h, qi: (b, k_col + h)),
            pl.BlockSpec((seq, gw), lambda b, h, qi: (b, v_col + h)),
            table_spec, table_spec, table_spec,
        ],
        out_specs=pl.BlockSpec((MOBA_BLOCK, gw), lambda b, h, qi: (b * n_blk + qi, h)),
        out_shape=jax.ShapeDtypeStruct((t, heads * HEAD_DIM), BF16),
        scratch_shapes=[
            pltpu.VMEM((hg, seq, HEAD_DIM), BF16),
            pltpu.VMEM((hg, n_blk, HEAD_DIM, MOBA_BLOCK), BF16),
            pltpu.VMEM((hg, n_blk // MOBA_SWEEP, HEAD_DIM, MOBA_SWEEP * MOBA_BLOCK), BF16),
            pltpu.VMEM((hg, n_blk, HEAD_DIM), F32),
            pltpu.VMEM((hg, n_blk, MOBA_BLOCK), F32),
            pltpu.VMEM((hg, MOBA_SWEEP * MOBA_BLOCK, MOBA_BLOCK), F32),
        ],
        compiler_params=_cparams(("arbitrary", "arbitrary", "arbitrary"), 48),
        name="moba_attention",
    )(proj, proj, proj, *tables)


def _deepnorm_ln(x, y, ep, ln_g, ln_b, alpha):
    r = alpha * x + (1.0 + ep[0:1]) * y
    mu = jnp.mean(r, axis=-1, keepdims=True)
    rc = r - mu
    var = jnp.mean(rc * rc, axis=-1, keepdims=True)
    return rc * lax.rsqrt(var + LN_EPS) * ln_g + ln_b


def _next_h(x_new, ep):
    return x_new * (1.0 + ep[1:2]) + ep[2:3]


def _mix_kernel(alpha, ya_ref, at_ref, ga0_ref, ga1_ref, gb0_ref, gb1_ref, x_ref, ep_ref,
                wpa_ref, wpb_ref, wout_ref, lng_ref, lnb_ref, xo_ref, ho_ref):
    y_a = jnp.dot(ya_ref[...], wpa_ref[...], preferred_element_type=F32)
    y_b = jnp.dot(at_ref[...], wpb_ref[...], preferred_element_type=F32)
    gate_a = jnp.concatenate([ga0_ref[...], ga1_ref[...]], axis=1).astype(F32)
    gate_b = jnp.concatenate([gb0_ref[...], gb1_ref[...]], axis=1).astype(F32)
    merged = jax.nn.sigmoid(gate_a) * y_a + jax.nn.sigmoid(gate_b) * y_b
    y = jnp.dot(merged.astype(BF16), wout_ref[...], preferred_element_type=F32)
    ep = ep_ref[0]
    x_new = _deepnorm_ln(x_ref[...], y, ep, lng_ref[...], lnb_ref[...], alpha)
    xo_ref[...] = x_new
    ho_ref[...] = _next_h(x_new, ep).astype(ho_ref.dtype)


def _mix(ya, attn, proj, x2d, ep, w_pa, w_pb, w_out, ln_g, ln_b, alpha, seq, ga_col, gb_col, tm=256):
    t, d = x2d.shape
    wa = ya.shape[1]
    half = d // 2
    per_b = seq // tm
    const = lambda i: (0, 0)
    return pl.pallas_call(
        functools.partial(_mix_kernel, alpha),
        grid=(t // tm,),
        in_specs=[
            pl.BlockSpec((tm, wa), lambda i: (i, 0)),
            pl.BlockSpec((tm, wa), lambda i: (i, 0)),
            pl.BlockSpec((tm, half), lambda i: (i, ga_col)),
            pl.BlockSpec((tm, half), lambda i: (i, ga_col + 1)),
            pl.BlockSpec((tm, half), lambda i: (i, gb_col)),
            pl.BlockSpec((tm, half), lambda i: (i, gb_col + 1)),
            pl.BlockSpec((tm, d), lambda i: (i, 0)),
            pl.BlockSpec((1, 3, d), lambda i: (i // per_b, 0, 0)),
            _resident((wa, d), const),
            _resident((wa, d), const),
            _resident((d, d), const),
            _resident((1, d), const),
            _resident((1, d), const),
        ],
        out_specs=[pl.BlockSpec((tm, d), lambda i: (i, 0)), pl.BlockSpec((tm, d), lambda i: (i, 0))],
        out_shape=[jax.ShapeDtypeStruct((t, d), F32), jax.ShapeDtypeStruct((t, d), BF16)],
        compiler_params=_cparams(("arbitrary",), 56),
        name="mix_out_ln",
    )(ya, attn, proj, proj, proj, proj, x2d, ep, w_pa, w_pb, w_out, ln_g.reshape(1, d),
      ln_b.reshape(1, d))


def _swiglu_step(h, wg, wu, wd):
    g = jnp.dot(h, wg, preferred_element_type=F32)
    u = jnp.dot(h, wu, preferred_element_type=F32)
    return jnp.dot((g * jax.nn.sigmoid(g) * u).astype(BF16), wd, preferred_element_type=F32)


def _ffn_kernel(alpha, h_ref, wg_ref, wu_ref, wd_ref, x_ref, ep_ref, lng_ref, lnb_ref,
                xo_ref, ho_ref, acc_ref):
    f = pl.program_id(1)

    @pl.when(f == 0)
    def _():
        acc_ref[...] = jnp.zeros_like(acc_ref)

    acc_ref[...] += _swiglu_step(h_ref[...], wg_ref[...], wu_ref[...], wd_ref[...])

    @pl.when(f == pl.num_programs(1) - 1)
    def _():
        ep = ep_ref[0]
        x_new = _deepnorm_ln(x_ref[...], acc_ref[...], ep, lng_ref[...], lnb_ref[...], alpha)
        xo_ref[...] = x_new
        ho_ref[...] = _next_h(x_new, ep).astype(ho_ref.dtype)


def _ffn(h, x2d, ep, w_gate, w_up, w_down, ln_g, ln_b, alpha, seq, tm=512, tf=512):
    t, d = x2d.shape
    ff = w_gate.shape[1]
    per_b = seq // tm
    return pl.pallas_call(
        functools.partial(_ffn_kernel, alpha),
        grid=(t // tm, ff // tf),
        in_specs=[
            pl.BlockSpec((tm, d), lambda i, f: (i, 0)),
            pl.BlockSpec((d, tf), lambda i, f: (0, f)),
            pl.BlockSpec((d, tf), lambda i, f: (0, f)),
            pl.BlockSpec((tf, d), lambda i, f: (f, 0)),
            pl.BlockSpec((tm, d), lambda i, f: (i, 0)),
            pl.BlockSpec((1, 3, d), lambda i, f: (i // per_b, 0, 0)),
            pl.BlockSpec((1, d), lambda i, f: (0, 0)),
            pl.BlockSpec((1, d), lambda i, f: (0, 0)),
        ],
        out_specs=[pl.BlockSpec((tm, d), lambda i, f: (i, 0)), pl.BlockSpec((tm, d), lambda i, f: (i, 0))],
        out_shape=[jax.ShapeDtypeStruct((t, d), F32), jax.ShapeDtypeStruct((t, d), BF16)],
        scratch_shapes=[pltpu.VMEM((tm, d), F32)],
        compiler_params=_cparams(("arbitrary", "arbitrary"), 56),
        name="ffn_dense_ln",
    )(h, w_gate, w_up, w_down, x2d, ep, ln_g.reshape(1, d), ln_b.reshape(1, d))


def _router_kernel(h_ref, w_ref, b_ref, o_ref, hp_ref):
    logits = jnp.dot(h_ref[...], w_ref[...], preferred_element_type=F32) + b_ref[...]
    lane_i = lax.broadcasted_iota(jnp.int32, logits.shape, 1)
    lane = lane_i.astype(F32)
    lg = jnp.where(lane_i < N_EXPERTS, logits, -jnp.inf)
    m1 = jnp.max(lg, axis=1, keepdims=True)
    i1 = jnp.min(jnp.where(lg == m1, lane, float(LANES)), axis=1, keepdims=True)
    lg2 = jnp.where(lane == i1, -jnp.inf, lg)
    m2 = jnp.max(lg2, axis=1, keepdims=True)
    i2 = jnp.min(jnp.where(lg2 == m2, lane, float(LANES)), axis=1, keepdims=True)
    e2 = jnp.exp(m2 - m1)
    denom = 1.0 + e2
    out = jnp.where(lane_i == 0, i1, 0.0)
    out = jnp.where(lane_i == 1, i2, out)
    out = jnp.where(lane_i == 2, 1.0 / denom, out)
    out = jnp.where(lane_i == 3, e2 / denom, out)
    o_ref[...] = out
    hp_ref[...] = _pack_bf16_pairs(h_ref[...])


def _pack_bf16_pairs(h):
    half = h.shape[1] // 2
    bits = pltpu.bitcast(h.astype(F32), jnp.uint32)
    return bits[:, :half] | lax.shift_right_logical(bits[:, half:], jnp.uint32(16))


def _unpack_bf16_pairs(packed):
    hi = pltpu.bitcast(packed & jnp.uint32(0xFFFF0000), F32)
    lo = pltpu.bitcast(lax.shift_left(packed, jnp.uint32(16)), F32)
    return jnp.concatenate([hi, lo], axis=1).astype(BF16)


def _router(h, w_router_pad, b_router_pad, tm=1024):
    t, d = h.shape
    return pl.pallas_call(
        _router_kernel,
        grid=(t // tm,),
        in_specs=[
            pl.BlockSpec((tm, d), lambda i: (i, 0)),
            pl.BlockSpec((d, LANES), lambda i: (0, 0)),
            pl.BlockSpec((1, LANES), lambda i: (0, 0)),
        ],
        out_specs=[pl.BlockSpec((tm, LANES), lambda i: (i, 0)),
                   pl.BlockSpec((tm, d // 2), lambda i: (i, 0))],
        out_shape=[jax.ShapeDtypeStruct((t, LANES), F32),
                   jax.ShapeDtypeStruct((t, d // 2), jnp.uint32)],
        compiler_params=_cparams(("arbitrary",), 32),
        name="moe_router",
    )(h, w_router_pad, b_router_pad)


def _gmm_kernel(te_ref, tv_ref, src_ref, dst_ref, hp_ref, wg_ref, wu_ref, wd_ref, y_ref,
                xbuf_ref, x_ref, acc_ref, gather_sem, scatter_sem):
    i = pl.program_id(0)
    f = pl.program_id(1)
    n_tiles = pl.num_programs(0)
    last_f = pl.num_programs(1) - 1
    tm = acc_ref.shape[0]
    valid = tv_ref[i] > 0
    slot = lax.rem(i, 2)

    def gather_start(tile, to_slot):
        base = tile * tm

        def row(r, carry):
            pltpu.make_async_copy(hp_ref.at[pl.ds(src_ref[base + r], 1), :],
                                  xbuf_ref.at[to_slot, pl.ds(r, 1), :],
                                  gather_sem.at[to_slot]).start()
            return carry

        lax.fori_loop(0, tm, row, 0, unroll=8)

    def gather_wait(from_slot):
        pltpu.make_async_copy(hp_ref.at[pl.ds(0, tm), :], xbuf_ref.at[from_slot],
                              gather_sem.at[from_slot]).wait()

    def scatter_start(tile):
        base = tile * tm

        def row(r, carry):
            pltpu.make_async_copy(acc_ref.at[pl.ds(r, 1), :],
                                  y_ref.at[pl.ds(dst_ref[base + r], 1), :],
                                  scatter_sem.at[0]).start()
            return carry

        lax.fori_loop(0, tm, row, 0, unroll=8)

    def scatter_wait():
        pltpu.make_async_copy(acc_ref, y_ref.at[pl.ds(0, tm), :], scatter_sem.at[0]).wait()

    prev_valid = jnp.logical_and(i > 0, tv_ref[jnp.maximum(i - 1, 0)] > 0)

    @pl.when(f == 0)
    def _():
        @pl.when(i == 0)
        def _():
            gather_start(0, 0)

        @pl.when(valid)
        def _():
            gather_wait(slot)
            x_ref[...] = _unpack_bf16_pairs(xbuf_ref[slot])

        nxt = jnp.minimum(i + 1, n_tiles - 1)

        @pl.when(jnp.logical_and(i + 1 < n_tiles, tv_ref[nxt] > 0))
        def _():
            gather_start(nxt, 1 - slot)

        @pl.when(jnp.logical_and(jnp.logical_not(valid), prev_valid))
        def _():
            scatter_wait()

    @pl.when(valid)
    def _():
        res = _swiglu_step(x_ref[...], wg_ref[...].astype(BF16), wu_ref[...].astype(BF16),
                           wd_ref[...].astype(BF16))

        @pl.when(f == 0)
        def _():
            @pl.when(prev_valid)
            def _():
                scatter_wait()

            acc_ref[...] = res

        @pl.when(f > 0)
        def _():
            acc_ref[...] += res

        @pl.when(f == last_f)
        def _():
            scatter_start(i)

            @pl.when(i == n_tiles - 1)
            def _():
                scatter_wait()


def _gmm(tile_expert, tile_valid, src_token, dst_row, hp, w_gate, w_up, w_down, layer, n_out_rows,
         tm, tf=256):
    r = src_token.shape[0]
    d = w_gate.shape[2]
    ff = w_gate.shape[3]
    nf = ff // tf

    def f_idx(i, f, tv):
        return jnp.where(tv[i] > 0, f, nf - 1)

    grid_spec = pltpu.PrefetchScalarGridSpec(
        num_scalar_prefetch=4,
        grid=(r // tm, nf),
        in_specs=[
            pl.BlockSpec(memory_space=pl.ANY),
            pl.BlockSpec((None, None, d, tf),
                         lambda i, f, te, tv, src, dst: (layer, te[i], 0, f_idx(i, f, tv))),
            pl.BlockSpec((None, None, d, tf),
                         lambda i, f, te, tv, src, dst: (layer, te[i], 0, f_idx(i, f, tv))),
            pl.BlockSpec((None, None, tf, d),
                         lambda i, f, te, tv, src, dst: (layer, te[i], f_idx(i, f, tv), 0)),
        ],
        out_specs=pl.BlockSpec(memory_space=pl.ANY),
        scratch_shapes=[
            pltpu.VMEM((2, tm, d // 2), jnp.uint32),
            pltpu.VMEM((tm, d), BF16),
            pltpu.VMEM((tm, d), F32),
            pltpu.SemaphoreType.DMA((2,)),
            pltpu.SemaphoreType.DMA((1,)),
        ],
    )
    return pl.pallas_call(
        _gmm_kernel,
        grid_spec=grid_spec,
        out_shape=jax.ShapeDtypeStruct((n_out_rows, d), F32),
        compiler_params=pltpu.CompilerParams(
            dimension_semantics=("arbitrary", "arbitrary"), vmem_limit_bytes=56 * MIB,
            disable_bounds_checks=True),
        name="moe_grouped_swiglu",
    )(tile_expert, tile_valid, src_token, dst_row, hp, w_gate, w_up, w_down)


def _combine_ln_kernel(alpha, route_ref, y1_ref, y2_ref, x_ref, ep_ref, lng_ref, lnb_ref,
                       xo_ref, ho_ref):
    ep = ep_ref[0]
    route = route_ref[...]
    y = route[:, 2:3] * y1_ref[...] + route[:, 3:4] * y2_ref[...]
    x_new = _deepnorm_ln(x_ref[...], y, ep, lng_ref[...], lnb_ref[...], alpha)
    xo_ref[...] = x_new
    ho_ref[...] = _next_h(x_new, ep).astype(ho_ref.dtype)


def _combine_ln(route, y_slots, x2d, ep, ln_g, ln_b, alpha, seq, tm=256):
    t, d = x2d.shape
    per_b = seq // tm
    n_tiles = t // tm
    return pl.pallas_call(
        functools.partial(_combine_ln_kernel, alpha),
        grid=(n_tiles,),
        in_specs=[
            pl.BlockSpec((tm, LANES), lambda i: (i, 0)),
            pl.BlockSpec((tm, d), lambda i: (i, 0)),
            pl.BlockSpec((tm, d), lambda i: (i + n_tiles, 0)),
            pl.BlockSpec((tm, d), lambda i: (i, 0)),
            pl.BlockSpec((1, 3, d), lambda i: (i // per_b, 0, 0)),
            pl.BlockSpec((1, d), lambda i: (0, 0)),
            pl.BlockSpec((1, d), lambda i: (0, 0)),
        ],
        out_specs=[pl.BlockSpec((tm, d), lambda i: (i, 0)), pl.BlockSpec((tm, d), lambda i: (i, 0))],
        out_shape=[jax.ShapeDtypeStruct((t, d), F32), jax.ShapeDtypeStruct((t, d), BF16)],
        compiler_params=_cparams(("arbitrary",), 48),
        name="moe_combine_ln",
    )(route, y_slots, y_slots, x2d, ep, ln_g.reshape(1, d), ln_b.reshape(1, d))


def _dispatch_plan(route, tm):
    t = route.shape[0]
    n_pairs = 2 * t
    n_rows = n_pairs + N_EXPERTS * tm
    expert = route[:, :2].astype(jnp.int32).reshape(n_pairs)
    onehot = (expert[:, None] == jnp.arange(N_EXPERTS, dtype=jnp.int32)[None, :]).astype(jnp.int32)
    rank = jnp.take_along_axis(jnp.cumsum(onehot, axis=0) - onehot, expert[:, None], axis=1)[:, 0]
    counts = jnp.sum(onehot, axis=0)
    padded = ((counts + tm - 1) // tm) * tm
    ends = jnp.cumsum(padded)
    starts = ends - padded
    pos = starts[expert] + rank
    pair = jnp.full((n_rows,), -1, jnp.int32).at[pos].set(jnp.arange(n_pairs, dtype=jnp.int32))
    token = jnp.maximum(pair, 0) // 2
    is_pad = (pair < 0).astype(jnp.int32)
    src_token = jnp.where(pair >= 0, token, 0)
    dst_row = jnp.where(pair >= 0, (pair % 2) * t + token, n_pairs + jnp.cumsum(is_pad) - 1)
    tile_start = jnp.arange(n_rows // tm, dtype=jnp.int32) * tm
    tile_valid = (tile_start < ends[-1]).astype(jnp.int32)
    n_ended = jnp.sum((tile_start[:, None] >= ends[None, :]).astype(jnp.int32), axis=1)
    tile_expert = jnp.minimum(n_ended, N_EXPERTS - 1)
    last_valid_expert = jnp.max(jnp.where(tile_valid > 0, tile_expert, 0))
    tile_expert = jnp.where(tile_valid > 0, tile_expert, last_valid_expert)
    return src_token, dst_row, tile_expert, tile_valid


def kernel(x, c, w_in, conv_w, conv_b, w_rgate, b_rgate, w_igate, b_igate, lru_lambda, w_proj_a, w_proj_b, w_out, w_ada, b_ada, ln_mix_g, ln_mix_b, ln_ffn_g, ln_ffn_b, ffn_w_gate, ffn_w_up, ffn_w_down, moe_w_router, moe_b_router, moe_w_gate, moe_w_up, moe_w_down):
    batch, seq, d = x.shape
    depth = w_in.shape[0]
    t = batch * seq
    lru_w = conv_w.shape[2]
    attn_w = w_proj_b.shape[1]
    heads = attn_w // HEAD_DIM
    alpha = (2.0 * depth) ** 0.25
    moe_tm = 1024

    c_pad = jnp.zeros((SUBLANES, d), F32).at[:batch].set(c)
    ada = _ada_all_layers(c_pad, w_ada, b_ada)[:, :batch].reshape(depth, batch, 6, d)
    zeros_bd = jnp.zeros((batch, d), F32)

    def epilogue_params(gate, scale, shift):
        return jnp.stack([gate, scale, shift], axis=1)

    tables = _rope_tables(seq)
    x2d = x.reshape(t, d)
    h = _modulate(x2d, epilogue_params(zeros_bd, ada[0, :, 1], ada[0, :, 0]), seq)

    head_group_w = MOBA_HEADS_PER_STEP * HEAD_DIM
    q_col = 2 * lru_w // head_group_w
    k_col = q_col + attn_w // head_group_w
    v_col = k_col + attn_w // head_group_w
    ga_col = (2 * lru_w + 3 * attn_w) // (d // 2)
    gb_col = ga_col + 2

    for l in range(depth):
        sh1, sc1, g1, sh2, sc2, g2 = (ada[l, :, j] for j in range(6))
        proj = _matmul(h, w_in, l)
        ya = _lru(proj, conv_w[l], conv_b[l], w_rgate[l], b_rgate[l], w_igate[l], b_igate[l],
                  lru_lambda[l], batch, seq)
        attn = _moba(proj, tables, batch, seq, heads, q_col, k_col, v_col)
        x2d, h = _mix(ya, attn, proj, x2d, epilogue_params(g1, sc2, sh2),
                      w_proj_a[l].astype(BF16), w_proj_b[l].astype(BF16), w_out[l].astype(BF16),
                      ln_mix_g[l], ln_mix_b[l], alpha, seq, ga_col, gb_col)
        if l + 1 < depth:
            ep = epilogue_params(g2, ada[l + 1, :, 1], ada[l + 1, :, 0])
        else:
            ep = epilogue_params(g2, zeros_bd, zeros_bd)
        j = l // 2
        if l % 2 == 0:
            x2d, h = _ffn(h, x2d, ep, ffn_w_gate[j].astype(BF16), ffn_w_up[j].astype(BF16),
                          ffn_w_down[j].astype(BF16), ln_ffn_g[l], ln_ffn_b[l], alpha, seq)
        else:
            w_r_pad = jnp.zeros((d, LANES), BF16).at[:, :N_EXPERTS].set(moe_w_router[j].astype(BF16))
            b_r_pad = jnp.zeros((1, LANES), F32).at[0, :N_EXPERTS].set(moe_b_router[j])
            route, h_packed = _router(h, w_r_pad, b_r_pad)
            src_token, dst_row, tile_expert, tile_valid = _dispatch_plan(route, moe_tm)
            y_slots = _gmm(tile_expert, tile_valid, src_token, dst_row, h_packed, moe_w_gate,
                           moe_w_up, moe_w_down, j, src_token.shape[0], moe_tm)
            x2d, h = _combine_ln(route, y_slots, x2d, ep, ln_ffn_g[l], ln_ffn_b[l], alpha, seq)
    return x2d.reshape(batch, seq, d)
```

```python
import functools

import jax
import jax.numpy as jnp
from jax import lax
from jax.experimental import pallas as pl
from jax.experimental.pallas import tpu as pltpu

F32 = jnp.float32
BF16 = jnp.bfloat16

LRU_BLOCKS = 8
LRU_BLOCK_W = 128
CONV_WIDTH = 4
LRU_C = 8.0
HEAD_DIM = 128
MOBA_BLOCK = 256
MOBA_TOPK = 3
MOBA_HEADS_PER_STEP = 2
MOBA_SWEEP = 4
ROPE_THETA = 500000.0
ROT_DIM = HEAD_DIM // 4
ROT_HALF = ROT_DIM // 2
NEG_INF = -1e30
N_EXPERTS = 8
LN_EPS = 1e-5
LANES = 128
SUBLANES = 8
MIB = 1024 * 1024


def _cparams(semantics, vmem_mib):
    return pltpu.CompilerParams(dimension_semantics=semantics, vmem_limit_bytes=vmem_mib * MIB)


def _resident(block_shape, index_map):
    return pl.BlockSpec(block_shape, index_map, pipeline_mode=pl.Buffered(1))


def _ada_kernel(c_ref, w_ref, b_ref, o_ref):
    c = c_ref[...]
    c_act = c * jax.nn.sigmoid(c)
    o_ref[0] = jnp.dot(c_act, w_ref[0], preferred_element_type=F32) + b_ref[0]


def _ada_all_layers(c_pad, w_ada, b_ada, tn=1024):
    depth, d, n = w_ada.shape
    rows = c_pad.shape[0]
    return pl.pallas_call(
        _ada_kernel,
        grid=(depth, n // tn),
        in_specs=[
            pl.BlockSpec((rows, d), lambda l, j: (0, 0)),
            pl.BlockSpec((1, d, tn), lambda l, j: (l, 0, j)),
            pl.BlockSpec((1, 1, tn), lambda l, j: (l, 0, j)),
        ],
        out_specs=pl.BlockSpec((1, rows, tn), lambda l, j: (l, 0, j)),
        out_shape=jax.ShapeDtypeStruct((depth, rows, n), F32),
        compiler_params=_cparams(("arbitrary", "arbitrary"), 40),
        name="ada",
    )(c_pad, w_ada, b_ada.reshape(depth, 1, n))


def _modulate_kernel(x_ref, ep_ref, h_ref):
    ep = ep_ref[0]
    h_ref[...] = (x_ref[...] * (1.0 + ep[1:2]) + ep[2:3]).astype(h_ref.dtype)


def _modulate(x2d, ep, seq, tm=512):
    t, d = x2d.shape
    per_b = seq // tm
    return pl.pallas_call(
        _modulate_kernel,
        grid=(t // tm,),
        in_specs=[
            pl.BlockSpec((tm, d), lambda i: (i, 0)),
            pl.BlockSpec((1, 3, d), lambda i: (i // per_b, 0, 0)),
        ],
        out_specs=pl.BlockSpec((tm, d), lambda i: (i, 0)),
        out_shape=jax.ShapeDtypeStruct((t, d), BF16),
        compiler_params=_cparams(("arbitrary",), 32),
        name="modulate",
    )(x2d, ep)


def _matmul_kernel(a_ref, w_ref, o_ref, wb_ref):
    @pl.when(pl.program_id(1) == 0)
    def _():
        wb_ref[...] = w_ref[...].astype(wb_ref.dtype)

    o_ref[...] = jnp.dot(a_ref[...], wb_ref[...], preferred_element_type=F32).astype(o_ref.dtype)


def _matmul(a, w, layer, tm=1024, tn=1024):
    m, k = a.shape
    n = w.shape[2]
    return pl.pallas_call(
        _matmul_kernel,
        grid=(n // tn, m // tm),
        in_specs=[
            pl.BlockSpec((tm, k), lambda j, i: (i, 0)),
            pl.BlockSpec((None, k, tn), lambda j, i: (layer, 0, j)),
        ],
        out_specs=pl.BlockSpec((tm, tn), lambda j, i: (i, j)),
        out_shape=jax.ShapeDtypeStruct((m, n), BF16),
        scratch_shapes=[pltpu.VMEM((k, tn), BF16)],
        compiler_params=_cparams(("arbitrary", "arbitrary"), 48),
        name="in_proj",
    )(a, w)


def _gelu_tanh(x):
    return 0.5 * x * (1.0 + jnp.tanh(0.7978845608028654 * (x + 0.044715 * (x * x * x))))


def _lru_kernel(x_ref, g_ref, cw_ref, cb_ref, wr_ref, br_ref, wi_ref, bi_ref, lam_ref,
                o_ref, tail_ref, carry_ref):
    @pl.when(pl.program_id(1) == 0)
    def _():
        tail_ref[...] = jnp.zeros_like(tail_ref)
        carry_ref[...] = jnp.zeros_like(carry_ref)

    ts, w = x_ref.shape
    groups = ts // SUBLANES
    x3 = x_ref[...].astype(F32).reshape(groups, SUBLANES, w)
    xe = jnp.concatenate([tail_ref[...][None], x3], axis=0)
    sub = lax.broadcasted_iota(jnp.int32, (groups, SUBLANES, w), 1)

    y = cb_ref[...][None] + cw_ref[CONV_WIDTH - 1:CONV_WIDTH, :][None] * x3
    for k in range(1, CONV_WIDTH):
        rolled = pltpu.roll(xe, k, axis=1)
        shifted = jnp.where(sub < k, rolled[:-1], rolled[1:])
        y = y + cw_ref[CONV_WIDTH - 1 - k:CONV_WIDTH - k, :][None] * shifted
    tail_ref[...] = x3[groups - 1]

    xa = y.reshape(ts, w)
    xab = xa.astype(BF16)

    def block_diag(w_ref, b_ref):
        parts = [
            jnp.dot(xab[:, n * LRU_BLOCK_W:(n + 1) * LRU_BLOCK_W], w_ref[n],
                    preferred_element_type=F32)
            for n in range(LRU_BLOCKS)
        ]
        return jnp.concatenate(parts, axis=1) + b_ref[...]

    r = jax.nn.sigmoid(block_diag(wr_ref, br_ref))
    i = jax.nn.sigmoid(block_diag(wi_ref, bi_ref))
    neg_lam = -lam_ref[...]
    softplus = jnp.maximum(neg_lam, 0.0) + jnp.log1p(jnp.exp(-jnp.abs(neg_lam)))
    a = jnp.exp((-LRU_C) * r * softplus)
    u = jnp.sqrt(1.0 - a * a) * (i * xa)

    a3 = a.reshape(groups, SUBLANES, w)
    u3 = u.reshape(groups, SUBLANES, w)
    d = 1
    while d < SUBLANES:
        a_prev = pltpu.roll(a3, d, axis=1)
        u_prev = pltpu.roll(u3, d, axis=1)
        take = sub >= d
        u3 = jnp.where(take, a3 * u_prev + u3, u3)
        a3 = jnp.where(take, a3 * a_prev, a3)
        d *= 2
    carry = carry_ref[...]
    hs = []
    for gi in range(groups):
        hg = u3[gi] + a3[gi] * carry
        hs.append(hg)
        carry = hg[SUBLANES - 1:SUBLANES, :]
    carry_ref[...] = carry
    h = jnp.stack(hs, axis=0).reshape(ts, w)

    o_ref[...] = (h * _gelu_tanh(g_ref[...].astype(F32))).astype(o_ref.dtype)


def _lru(proj, conv_w, conv_b, w_r, b_r, w_i, b_i, lam, batch, seq, ts=256):
    t = proj.shape[0]
    w = conv_w.shape[1]
    per_b = seq // ts
    row = lambda b, s: b * per_b + s
    vec = lambda v: v.reshape(1, w)
    const2 = lambda b, s: (0, 0)
    const3 = lambda b, s: (0, 0, 0)
    return pl.pallas_call(
        _lru_kernel,
        grid=(batch, per_b),
        in_specs=[
            pl.BlockSpec((ts, w), lambda b, s: (row(b, s), 0)),
            pl.BlockSpec((ts, w), lambda b, s: (row(b, s), 1)),
            pl.BlockSpec((CONV_WIDTH, w), const2),
            pl.BlockSpec((1, w), const2),
            pl.BlockSpec((LRU_BLOCKS, LRU_BLOCK_W, LRU_BLOCK_W), const3),
            pl.BlockSpec((1, w), const2),
            pl.BlockSpec((LRU_BLOCKS, LRU_BLOCK_W, LRU_BLOCK_W), const3),
            pl.BlockSpec((1, w), const2),
            pl.BlockSpec((1, w), const2),
        ],
        out_specs=pl.BlockSpec((ts, w), lambda b, s: (row(b, s), 0)),
        out_shape=jax.ShapeDtypeStruct((t, w), BF16),
        scratch_shapes=[pltpu.VMEM((SUBLANES, w), F32), pltpu.VMEM((1, w), F32)],
        compiler_params=_cparams(("arbitrary", "arbitrary"), 48),
        name="conv_rglru",
    )(proj, proj, conv_w, vec(conv_b), w_r.astype(BF16), vec(b_r), w_i.astype(BF16), vec(b_i),
      vec(lam))


def _rotary(x, cos_ref, sin_lo_ref, sin_hi_ref, rows):
    up = pltpu.roll(x, HEAD_DIM - ROT_HALF, axis=1)
    down = pltpu.roll(x, ROT_HALF, axis=1)
    return x * cos_ref[rows, :] + up * sin_lo_ref[rows, :] + down * sin_hi_ref[rows, :]


_NT = (((1,), (1,)), ((), ()))


def _moba_kernel(q_ref, k_ref, v_ref, cos_ref, sin_lo_ref, sin_hi_ref, o_ref,
                 krot_ref, vt_ref, vtg_ref, kmean_ref, sel_ref, raw_ref):
    qi = pl.program_id(2)
    n_heads, n_blk = kmean_ref.shape[0], kmean_ref.shape[1]
    scale2 = HEAD_DIM ** -0.5 * 1.4426950408889634
    head_cols = [slice(hh * HEAD_DIM, (hh + 1) * HEAD_DIM) for hh in range(n_heads)]

    @pl.when(qi == 0)
    def _():
        for hh in range(n_heads):
            for n in range(n_blk):
                rows = pl.ds(n * MOBA_BLOCK, MOBA_BLOCK)
                kr = _rotary(k_ref[rows, head_cols[hh]].astype(F32), cos_ref, sin_lo_ref,
                             sin_hi_ref, rows)
                krot_ref[hh, rows, :] = kr.astype(krot_ref.dtype)
                kmean_ref[hh, n:n + 1, :] = jnp.mean(kr, axis=0, keepdims=True)
                vt = v_ref[rows, head_cols[hh]].astype(F32).T.astype(vt_ref.dtype)
                vt_ref[hh, n] = vt
                c = n % MOBA_SWEEP
                vtg_ref[hh, n // MOBA_SWEEP, :, c * MOBA_BLOCK:(c + 1) * MOBA_BLOCK] = vt

    own_rows = pl.ds(pl.multiple_of(qi * MOBA_BLOCK, MOBA_BLOCK), MOBA_BLOCK)

    qts, init = [], []
    for hh in range(n_heads):
        qt = _rotary(q_ref[:, head_cols[hh]].astype(F32), cos_ref, sin_lo_ref, sin_hi_ref,
                     own_rows).T.astype(BF16)
        qts.append(qt)

        gate = jnp.dot(kmean_ref[hh].astype(BF16), qt, preferred_element_type=F32)
        blk_i = lax.broadcasted_iota(jnp.int32, gate.shape, 0)
        blk = blk_i.astype(F32)
        past = blk_i < qi
        g = jnp.where(past, gate, -jnp.inf)
        sel = jnp.zeros_like(gate)
        for _ in range(MOBA_TOPK):
            best = jnp.max(g, axis=0, keepdims=True)
            idx = jnp.min(jnp.where(g == best, blk, float(n_blk)), axis=0, keepdims=True)
            pick = blk == idx
            sel = jnp.where(pick, 1.0, sel)
            g = jnp.where(pick, -jnp.inf, g)
        sel_ref[hh] = jnp.where(past, sel, 0.0)

        s = jnp.dot(krot_ref[hh, own_rows, :], qt, preferred_element_type=F32) * scale2
        key_id = lax.broadcasted_iota(jnp.int32, s.shape, 0)
        qry_id = lax.broadcasted_iota(jnp.int32, s.shape, 1)
        s = jnp.where(key_id <= qry_id, s, NEG_INF)
        m0 = jnp.max(s, axis=0, keepdims=True)
        p = jnp.exp2(s - m0)
        l0 = jnp.sum(p, axis=0, keepdims=True)
        acc0 = jnp.dot(vt_ref[hh, qi], p.astype(BF16), preferred_element_type=F32)
        init.append((m0, l0, acc0))

    span = MOBA_SWEEP * MOBA_BLOCK
    last_group = n_blk // MOBA_SWEEP - 1

    def raw_scores(hh, j):
        rows = pl.ds(pl.multiple_of(j * span, span), span)
        return jnp.dot(krot_ref[hh, rows, :], qts[hh], preferred_element_type=F32)

    for hh in range(n_heads):
        raw_ref[hh] = raw_scores(hh, 0)

    def past_group(j, carry):
        out = []
        for hh in range(n_heads):
            m, l, acc = carry[hh]
            s = jnp.concatenate([
                jnp.where(sel_ref[hh, pl.ds(j * MOBA_SWEEP + c, 1), :] > 0.0,
                          raw_ref[hh, c * MOBA_BLOCK:(c + 1) * MOBA_BLOCK, :] * scale2, NEG_INF)
                for c in range(MOBA_SWEEP)], axis=0)
            raw_ref[hh] = raw_scores(hh, jnp.minimum(j + 1, last_group))
            m_new = jnp.maximum(m, jnp.max(s, axis=0, keepdims=True))
            alpha = jnp.exp2(m - m_new)
            p = jnp.exp2(s - m_new)
            l = alpha * l + jnp.sum(p, axis=0, keepdims=True)
            acc = alpha * acc + jnp.dot(vtg_ref[hh, j], p.astype(BF16), preferred_element_type=F32)
            out.append((m_new, l, acc))
        return tuple(out)

    n_groups = lax.div(qi + (MOBA_SWEEP - 1), MOBA_SWEEP)
    final = lax.fori_loop(0, n_groups, past_group, tuple(init))
    for hh in range(n_heads):
        _, l, acc = final[hh]
        o_ref[:, head_cols[hh]] = (acc / l).T.astype(o_ref.dtype)


def _rope_tables(seq):
    pos = jnp.arange(seq, dtype=F32)
    inv_freq = ROPE_THETA ** (-jnp.arange(0, ROT_DIM, 2, dtype=F32) / ROT_DIM)
    ang = pos[:, None] * inv_freq[None, :]
    cos, sin = jnp.cos(ang), jnp.sin(ang)
    rest = HEAD_DIM - ROT_DIM
    cos_t = jnp.concatenate([cos, cos, jnp.ones((seq, rest), F32)], axis=1)
    sin_lo = jnp.concatenate([-sin, jnp.zeros((seq, HEAD_DIM - ROT_HALF), F32)], axis=1)
    sin_hi = jnp.concatenate([jnp.zeros((seq, ROT_HALF), F32), sin, jnp.zeros((seq, rest), F32)], axis=1)
    return cos_t, sin_lo, sin_hi


def _moba(proj, tables, batch, seq, heads, q_col, k_col, v_col, hg=MOBA_HEADS_PER_STEP):
    t = proj.shape[0]
    n_blk = seq // MOBA_BLOCK
    gw = hg * HEAD_DIM
    table_spec = _resident((seq, HEAD_DIM), lambda b, h, qi: (0, 0))
    return pl.pallas_call(
        _moba_kernel,
        grid=(batch, heads // hg, n_blk),
        in_specs=[
            pl.BlockSpec((MOBA_BLOCK, gw), lambda b, h, qi: (b * n_blk + qi, q_col + h)),
            pl.BlockSpec((seq, gw), lambda b, h, qi: (b, k_col + h)),
            pl.BlockSpec((seq, gw), lambda b, h, qi: (b, v_col + h)),
            table_spec, table_spec, table_spec,
        ],
        out_specs=pl.BlockSpec((MOBA_BLOCK, gw), lambda b, h, qi: (b * n_blk + qi, h)),
        out_shape=jax.ShapeDtypeStruct((t, heads * HEAD_DIM), BF16),
        scratch_shapes=[
            pltpu.VMEM((hg, seq, HEAD_DIM), BF16),
            pltpu.VMEM((hg, n_blk, HEAD_DIM, MOBA_BLOCK), BF16),
            pltpu.VMEM((hg, n_blk // MOBA_SWEEP, HEAD_DIM, MOBA_SWEEP * MOBA_BLOCK), BF16),
            pltpu.VMEM((hg, n_blk, HEAD_DIM), F32),
            pltpu.VMEM((hg, n_blk, MOBA_BLOCK), F32),
            pltpu.VMEM((hg, MOBA_SWEEP * MOBA_BLOCK, MOBA_BLOCK), F32),
        ],
        compiler_params=_cparams(("arbitrary", "arbitrary", "arbitrary"), 48),
        name="moba_attention",
    )(proj, proj, proj, *tables)


def _deepnorm_ln(x, y, ep, ln_g, ln_b, alpha):
    r = alpha * x + (1.0 + ep[0:1]) * y
    mu = jnp.mean(r, axis=-1, keepdims=True)
    rc = r - mu
    var = jnp.mean(rc * rc, axis=-1, keepdims=True)
    return rc * lax.rsqrt(var + LN_EPS) * ln_g + ln_b


def _next_h(x_new, ep):
    return x_new * (1.0 + ep[1:2]) + ep[2:3]


def _mix_kernel(alpha, ya_ref, at_ref, ga0_ref, ga1_ref, gb0_ref, gb1_ref, x_ref, ep_ref,
                wpa_ref, wpb_ref, wout_ref, lng_ref, lnb_ref, xo_ref, ho_ref):
    y_a = jnp.dot(ya_ref[...], wpa_ref[...], preferred_element_type=F32)
    y_b = jnp.dot(at_ref[...], wpb_ref[...], preferred_element_type=F32)
    gate_a = jnp.concatenate([ga0_ref[...], ga1_ref[...]], axis=1).astype(F32)
    gate_b = jnp.concatenate([gb0_ref[...], gb1_ref[...]], axis=1).astype(F32)
    merged = jax.nn.sigmoid(gate_a) * y_a + jax.nn.sigmoid(gate_b) * y_b
    y = jnp.dot(merged.astype(BF16), wout_ref[...], preferred_element_type=F32)
    ep = ep_ref[0]
    x_new = _deepnorm_ln(x_ref[...], y, ep, lng_ref[...], lnb_ref[...], alpha)
    xo_ref[...] = x_new
    ho_ref[...] = _next_h(x_new, ep).astype(ho_ref.dtype)


def _mix(ya, attn, proj, x2d, ep, w_pa, w_pb, w_out, ln_g, ln_b, alpha, seq, ga_col, gb_col, tm=256):
    t, d = x2d.shape
    wa = ya.shape[1]
    half = d // 2
    per_b = seq // tm
    const = lambda i: (0, 0)
    return pl.pallas_call(
        functools.partial(_mix_kernel, alpha),
        grid=(t // tm,),
        in_specs=[
            pl.BlockSpec((tm, wa), lambda i: (i, 0)),
            pl.BlockSpec((tm, wa), lambda i: (i, 0)),
            pl.BlockSpec((tm, half), lambda i: (i, ga_col)),
            pl.BlockSpec((tm, half), lambda i: (i, ga_col + 1)),
            pl.BlockSpec((tm, half), lambda i: (i, gb_col)),
            pl.BlockSpec((tm, half), lambda i: (i, gb_col + 1)),
            pl.BlockSpec((tm, d), lambda i: (i, 0)),
            pl.BlockSpec((1, 3, d), lambda i: (i // per_b, 0, 0)),
            _resident((wa, d), const),
            _resident((wa, d), const),
            _resident((d, d), const),
            _resident((1, d), const),
            _resident((1, d), const),
        ],
        out_specs=[pl.BlockSpec((tm, d), lambda i: (i, 0)), pl.BlockSpec((tm, d), lambda i: (i, 0))],
        out_shape=[jax.ShapeDtypeStruct((t, d), F32), jax.ShapeDtypeStruct((t, d), BF16)],
        compiler_params=_cparams(("arbitrary",), 56),
        name="mix_out_ln",
    )(ya, attn, proj, proj, proj, proj, x2d, ep, w_pa, w_pb, w_out, ln_g.reshape(1, d),
      ln_b.reshape(1, d))


def _swiglu_step(h, wg, wu, wd):
    g = jnp.dot(h, wg, preferred_element_type=F32)
    u = jnp.dot(h, wu, preferred_element_type=F32)
    return jnp.dot((g * jax.nn.sigmoid(g) * u).astype(BF16), wd, preferred_element_type=F32)


def _ffn_kernel(alpha, h_ref, wg_ref, wu_ref, wd_ref, x_ref, ep_ref, lng_ref, lnb_ref,
                xo_ref, ho_ref, acc_ref):
    f = pl.program_id(1)

    @pl.when(f == 0)
    def _():
        acc_ref[...] = jnp.zeros_like(acc_ref)

    acc_ref[...] += _swiglu_step(h_ref[...], wg_ref[...], wu_ref[...], wd_ref[...])

    @pl.when(f == pl.num_programs(1) - 1)
    def _():
        ep = ep_ref[0]
        x_new = _deepnorm_ln(x_ref[...], acc_ref[...], ep, lng_ref[...], lnb_ref[...], alpha)
        xo_ref[...] = x_new
        ho_ref[...] = _next_h(x_new, ep).astype(ho_ref.dtype)


def _ffn(h, x2d, ep, w_gate, w_up, w_down, ln_g, ln_b, alpha, seq, tm=512, tf=512):
    t, d = x2d.shape
    ff = w_gate.shape[1]
    per_b = seq // tm
    return pl.pallas_call(
        functools.partial(_ffn_kernel, alpha),
        grid=(t // tm, ff // tf),
        in_specs=[
            pl.BlockSpec((tm, d), lambda i, f: (i, 0)),
            pl.BlockSpec((d, tf), lambda i, f: (0, f)),
            pl.BlockSpec((d, tf), lambda i, f: (0, f)),
            pl.BlockSpec((tf, d), lambda i, f: (f, 0)),
            pl.BlockSpec((tm, d), lambda i, f: (i, 0)),
            pl.BlockSpec((1, 3, d), lambda i, f: (i // per_b, 0, 0)),
            pl.BlockSpec((1, d), lambda i, f: (0, 0)),
            pl.BlockSpec((1, d), lambda i, f: (0, 0)),
        ],
        out_specs=[pl.BlockSpec((tm, d), lambda i, f: (i, 0)), pl.BlockSpec((tm, d), lambda i, f: (i, 0))],
        out_shape=[jax.ShapeDtypeStruct((t, d), F32), jax.ShapeDtypeStruct((t, d), BF16)],
        scratch_shapes=[pltpu.VMEM((tm, d), F32)],
        compiler_params=_cparams(("arbitrary", "arbitrary"), 56),
        name="ffn_dense_ln",
    )(h, w_gate, w_up, w_down, x2d, ep, ln_g.reshape(1, d), ln_b.reshape(1, d))


def _router_kernel(h_ref, w_ref, b_ref, o_ref, hp_ref):
    logits = jnp.dot(h_ref[...], w_ref[...], preferred_element_type=F32) + b_ref[...]
    lane_i = lax.broadcasted_iota(jnp.int32, logits.shape, 1)
    lane = lane_i.astype(F32)
    lg = jnp.where(lane_i < N_EXPERTS, logits, -jnp.inf)
    m1 = jnp.max(lg, axis=1, keepdims=True)
    i1 = jnp.min(jnp.where(lg == m1, lane, float(LANES)), axis=1, keepdims=True)
    lg2 = jnp.where(lane == i1, -jnp.inf, lg)
    m2 = jnp.max(lg2, axis=1, keepdims=True)
    i2 = jnp.min(jnp.where(lg2 == m2, lane, float(LANES)), axis=1, keepdims=True)
    e2 = jnp.exp(m2 - m1)
    denom = 1.0 + e2
    out = jnp.where(lane_i == 0, i1, 0.0)
    out = jnp.where(lane_i == 1, i2, out)
    out = jnp.where(lane_i == 2, 1.0 / denom, out)
    out = jnp.where(lane_i == 3, e2 / denom, out)
    o_ref[...] = out
    _store_token_tiles(hp_ref, _pack_bf16_pairs(h_ref[...]))


def _pack_bf16_pairs(h):
    half = h.shape[1] // 2
    bits = pltpu.bitcast(h.astype(F32), jnp.uint32)
    return bits[:, :half] | lax.shift_right_logical(bits[:, half:], jnp.uint32(16))


def _store_token_tiles(ref, rows):
    m = rows.shape[0]
    n = rows.shape[1] // LANES
    for c in range(n):
        ref[pl.ds(c, m, stride=n), :] = rows[:, c * LANES:(c + 1) * LANES]


def _load_token_tiles(ref, n):
    m = ref.shape[0] // n
    return jnp.concatenate([ref[pl.ds(c, m, stride=n), :] for c in range(n)], axis=1)


def _unpack_bf16_pairs(packed):
    hi = pltpu.bitcast(packed & jnp.uint32(0xFFFF0000), F32)
    lo = pltpu.bitcast(lax.shift_left(packed, jnp.uint32(16)), F32)
    return jnp.concatenate([hi, lo], axis=1).astype(BF16)


def _router(h, w_router_pad, b_router_pad, tm=1024):
    t, d = h.shape
    return pl.pallas_call(
        _router_kernel,
        grid=(t // tm,),
        in_specs=[
            pl.BlockSpec((tm, d), lambda i: (i, 0)),
            pl.BlockSpec((d, LANES), lambda i: (0, 0)),
            pl.BlockSpec((1, LANES), lambda i: (0, 0)),
        ],
        out_specs=[pl.BlockSpec((tm, LANES), lambda i: (i, 0)),
                   pl.BlockSpec((tm * (d // 2 // LANES), LANES), lambda i: (i, 0))],
        out_shape=[jax.ShapeDtypeStruct((t, LANES), F32),
                   jax.ShapeDtypeStruct((t * (d // 2 // LANES), LANES), jnp.uint32)],
        compiler_params=_cparams(("arbitrary",), 32),
        name="moe_router",
    )(h, w_router_pad, b_router_pad)


def _gmm_kernel(te_ref, tv_ref, src_ref, dst_ref, hp_ref, wg_ref, wu_ref, wd_ref, y_ref,
                xbuf_ref, x_ref, acc_ref, ybuf_ref, gather_sem, scatter_sem):
    i = pl.program_id(0)
    f = pl.program_id(1)
    n_tiles = pl.num_programs(0)
    last_f = pl.num_programs(1) - 1
    tm = acc_ref.shape[0]
    n_in = xbuf_ref.shape[1] // tm
    n_out = ybuf_ref.shape[0] // tm
    valid = tv_ref[i] > 0
    slot = lax.rem(i, 2)

    def tile_rows(row, n):
        return pl.ds(pl.multiple_of(row * n, n), n)

    def gather_start(tile, to_slot):
        base = tile * tm

        def row(r, carry):
            pltpu.make_async_copy(hp_ref.at[tile_rows(src_ref[base + r], n_in), :],
                                  xbuf_ref.at[to_slot, tile_rows(r, n_in), :],
                                  gather_sem.at[to_slot]).start()
            return carry

        lax.fori_loop(0, tm, row, 0, unroll=8)

    def gather_wait(from_slot):
        pltpu.make_async_copy(hp_ref.at[pl.ds(0, tm * n_in), :], xbuf_ref.at[from_slot],
                              gather_sem.at[from_slot]).wait()

    def scatter_start(tile):
        base = tile * tm

        def row(r, carry):
            pltpu.make_async_copy(ybuf_ref.at[tile_rows(r, n_out), :],
                                  y_ref.at[tile_rows(dst_ref[base + r], n_out), :],
                                  scatter_sem.at[0]).start()
            return carry

        lax.fori_loop(0, tm, row, 0, unroll=8)

    def scatter_wait():
        pltpu.make_async_copy(ybuf_ref, y_ref.at[pl.ds(0, tm * n_out), :],
                              scatter_sem.at[0]).wait()

    prev_valid = jnp.logical_and(i > 0, tv_ref[jnp.maximum(i - 1, 0)] > 0)

    @pl.when(f == 0)
    def _():
        @pl.when(i == 0)
        def _():
            gather_start(0, 0)

        @pl.when(valid)
        def _():
            gather_wait(slot)
            x_ref[...] = _unpack_bf16_pairs(_load_token_tiles(xbuf_ref.at[slot], n_in))

        nxt = jnp.minimum(i + 1, n_tiles - 1)

        @pl.when(jnp.logical_and(i + 1 < n_tiles, tv_ref[nxt] > 0))
        def _():
            gather_start(nxt, 1 - slot)

        @pl.when(jnp.logical_and(jnp.logical_not(valid), prev_valid))
        def _():
            scatter_wait()

    @pl.when(valid)
    def _():
        res = _swiglu_step(x_ref[...], wg_ref[...].astype(BF16), wu_ref[...].astype(BF16),
                           wd_ref[...].astype(BF16))

        @pl.when(f == 0)
        def _():
            acc_ref[...] = res

        @pl.when(f > 0)
        def _():
            acc_ref[...] += res

        @pl.when(f == last_f)
        def _():
            @pl.when(prev_valid)
            def _():
                scatter_wait()

            _store_token_tiles(ybuf_ref, acc_ref[...])
            scatter_start(i)

            @pl.when(i == n_tiles - 1)
            def _():
                scatter_wait()


def _gmm(tile_expert, tile_valid, src_token, dst_row, hp, w_gate, w_up, w_down, layer, n_out_rows,
         tm, tf=256):
    r = src_token.shape[0]
    d = w_gate.shape[2]
    ff = w_gate.shape[3]
    nf = ff // tf

    def f_idx(i, f, tv):
        return jnp.where(tv[i] > 0, f, nf - 1)

    grid_spec = pltpu.PrefetchScalarGridSpec(
        num_scalar_prefetch=4,
        grid=(r // tm, nf),
        in_specs=[
            pl.BlockSpec(memory_space=pl.ANY),
            pl.BlockSpec((None, None, d, tf),
                         lambda i, f, te, tv, src, dst: (layer, te[i], 0, f_idx(i, f, tv))),
            pl.BlockSpec((None, None, d, tf),
                         lambda i, f, te, tv, src, dst: (layer, te[i], 0, f_idx(i, f, tv))),
            pl.BlockSpec((None, None, tf, d),
                         lambda i, f, te, tv, src, dst: (layer, te[i], f_idx(i, f, tv), 0)),
        ],
        out_specs=pl.BlockSpec(memory_space=pl.ANY),
        scratch_shapes=[
            pltpu.VMEM((2, tm * (d // 2 // LANES), LANES), jnp.uint32),
            pltpu.VMEM((tm, d), BF16),
            pltpu.VMEM((tm, d), F32),
            pltpu.VMEM((tm * (d // LANES), LANES), F32),
            pltpu.SemaphoreType.DMA((2,)),
            pltpu.SemaphoreType.DMA((1,)),
        ],
    )
    return pl.pallas_call(
        _gmm_kernel,
        grid_spec=grid_spec,
        out_shape=jax.ShapeDtypeStruct((n_out_rows * (d // LANES), LANES), F32),
        compiler_params=pltpu.CompilerParams(
            dimension_semantics=("arbitrary", "arbitrary"), vmem_limit_bytes=56 * MIB,
            disable_bounds_checks=True),
        name="moe_grouped_swiglu",
    )(tile_expert, tile_valid, src_token, dst_row, hp, w_gate, w_up, w_down)


def _combine_ln_kernel(alpha, route_ref, y1_ref, y2_ref, x_ref, ep_ref, lng_ref, lnb_ref,
                       xo_ref, ho_ref):
    ep = ep_ref[0]
    route = route_ref[...]
    n = x_ref.shape[1] // LANES
    y = (route[:, 2:3] * _load_token_tiles(y1_ref, n)
         + route[:, 3:4] * _load_token_tiles(y2_ref, n))
    x_new = _deepnorm_ln(x_ref[...], y, ep, lng_ref[...], lnb_ref[...], alpha)
    xo_ref[...] = x_new
    ho_ref[...] = _next_h(x_new, ep).astype(ho_ref.dtype)


def _combine_ln(route, y_slots, x2d, ep, ln_g, ln_b, alpha, seq, tm=256):
    t, d = x2d.shape
    per_b = seq // tm
    n_tiles = t // tm
    return pl.pallas_call(
        functools.partial(_combine_ln_kernel, alpha),
        grid=(n_tiles,),
        in_specs=[
            pl.BlockSpec((tm, LANES), lambda i: (i, 0)),
            pl.BlockSpec((tm * (d // LANES), LANES), lambda i: (i, 0)),
            pl.BlockSpec((tm * (d // LANES), LANES), lambda i: (i + n_tiles, 0)),
            pl.BlockSpec((tm, d), lambda i: (i, 0)),
            pl.BlockSpec((1, 3, d), lambda i: (i // per_b, 0, 0)),
            pl.BlockSpec((1, d), lambda i: (0, 0)),
            pl.BlockSpec((1, d), lambda i: (0, 0)),
        ],
        out_specs=[pl.BlockSpec((tm, d), lambda i: (i, 0)), pl.BlockSpec((tm, d), lambda i: (i, 0))],
        out_shape=[jax.ShapeDtypeStruct((t, d), F32), jax.ShapeDtypeStruct((t, d), BF16)],
        compiler_params=_cparams(("arbitrary",), 48),
        name="moe_combine_ln",
    )(route, y_slots, y_slots, x2d, ep, ln_g.reshape(1, d), ln_b.reshape(1, d))


def _dispatch_plan(route, tm):
    t = route.shape[0]
    n_pairs = 2 * t
    n_rows = n_pairs + N_EXPERTS * tm
    expert = route[:, :2].astype(jnp.int32).reshape(n_pairs)
    onehot = (expert[:, None] == jnp.arange(N_EXPERTS, dtype=jnp.int32)[None, :]).astype(jnp.int32)
    rank = jnp.take_along_axis(jnp.cumsum(onehot, axis=0) - onehot, expert[:, None], axis=1)[:, 0]
    counts = jnp.sum(onehot, axis=0)
    padded = ((counts + tm - 1) // tm) * tm
    ends = jnp.cumsum(padded)
    starts = ends - padded
    pos = starts[expert] + rank
    pair = jnp.full((n_rows,), -1, jnp.int32).at[pos].set(jnp.arange(n_pairs, dtype=jnp.int32))
    token = jnp.maximum(pair, 0) // 2
    is_pad = (pair < 0).astype(jnp.int32)
    src_token = jnp.where(pair >= 0, token, 0)
    dst_row = jnp.where(pair >= 0, (pair % 2) * t + token, n_pairs + jnp.cumsum(is_pad) - 1)
    tile_start = jnp.arange(n_rows // tm, dtype=jnp.int32) * tm
    tile_valid = (tile_start < ends[-1]).astype(jnp.int32)
    n_ended = jnp.sum((tile_start[:, None] >= ends[None, :]).astype(jnp.int32), axis=1)
    tile_expert = jnp.minimum(n_ended, N_EXPERTS - 1)
    last_valid_expert = jnp.max(jnp.where(tile_valid > 0, tile_expert, 0))
    tile_expert = jnp.where(tile_valid > 0, tile_expert, last_valid_expert)
    return src_token, dst_row, tile_expert, tile_valid


def kernel(x, c, w_in, conv_w, conv_b, w_rgate, b_rgate, w_igate, b_igate, lru_lambda, w_proj_a, w_proj_b, w_out, w_ada, b_ada, ln_mix_g, ln_mix_b, ln_ffn_g, ln_ffn_b, ffn_w_gate, ffn_w_up, ffn_w_down, moe_w_router, moe_b_router, moe_w_gate, moe_w_up, moe_w_down):
    batch, seq, d = x.shape
    depth = w_in.shape[0]
    t = batch * seq
    lru_w = conv_w.shape[2]
    attn_w = w_proj_b.shape[1]
    heads = attn_w // HEAD_DIM
    alpha = (2.0 * depth) ** 0.25
    moe_tm = 1024

    c_pad = jnp.zeros((SUBLANES, d), F32).at[:batch].set(c)
    ada = _ada_all_layers(c_pad, w_ada, b_ada)[:, :batch].reshape(depth, batch, 6, d)
    zeros_bd = jnp.zeros((batch, d), F32)

    def epilogue_params(gate, scale, shift):
        return jnp.stack([gate, scale, shift], axis=1)

    tables = _rope_tables(seq)
    x2d = x.reshape(t, d)
    h = _modulate(x2d, epilogue_params(zeros_bd, ada[0, :, 1], ada[0, :, 0]), seq)

    head_group_w = MOBA_HEADS_PER_STEP * HEAD_DIM
    q_col = 2 * lru_w // head_group_w
    k_col = q_col + attn_w // head_group_w
    v_col = k_col + attn_w // head_group_w
    ga_col = (2 * lru_w + 3 * attn_w) // (d // 2)
    gb_col = ga_col + 2

    for l in range(depth):
        sh1, sc1, g1, sh2, sc2, g2 = (ada[l, :, j] for j in range(6))
        proj = _matmul(h, w_in, l)
        ya = _lru(proj, conv_w[l], conv_b[l], w_rgate[l], b_rgate[l], w_igate[l], b_igate[l],
                  lru_lambda[l], batch, seq)
        attn = _moba(proj, tables, batch, seq, heads, q_col, k_col, v_col)
        x2d, h = _mix(ya, attn, proj, x2d, epilogue_params(g1, sc2, sh2),
                      w_proj_a[l].astype(BF16), w_proj_b[l].astype(BF16), w_out[l].astype(BF16),
                      ln_mix_g[l], ln_mix_b[l], alpha, seq, ga_col, gb_col)
        if l + 1 < depth:
            ep = epilogue_params(g2, ada[l + 1, :, 1], ada[l + 1, :, 0])
        else:
            ep = epilogue_params(g2, zeros_bd, zeros_bd)
        j = l // 2
        if l % 2 == 0:
            x2d, h = _ffn(h, x2d, ep, ffn_w_gate[j].astype(BF16), ffn_w_up[j].astype(BF16),
                          ffn_w_down[j].astype(BF16), ln_ffn_g[l], ln_ffn_b[l], alpha, seq)
        else:
            w_r_pad = jnp.zeros((d, LANES), BF16).at[:, :N_EXPERTS].set(moe_w_router[j].astype(BF16))
            b_r_pad = jnp.zeros((1, LANES), F32).at[0, :N_EXPERTS].set(moe_b_router[j])
            route, h_packed = _router(h, w_r_pad, b_r_pad)
            src_token, dst_row, tile_expert, tile_valid = _dispatch_plan(route, moe_tm)
            y_slots = _gmm(tile_expert, tile_valid, src_token, dst_row, h_packed, moe_w_gate,
                           moe_w_up, moe_w_down, j, src_token.shape[0], moe_tm)
            x2d, h = _combine_ln(route, y_slots, x2d, ep, ln_ffn_g[l], ln_ffn_b[l], alpha, seq)
    return x2d.reshape(batch, seq, d)
```

```python
import functools

import jax
import jax.numpy as jnp
from jax import lax
from jax.experimental import pallas as pl
from jax.experimental.pallas import tpu as pltpu

F32 = jnp.float32
BF16 = jnp.bfloat16

LRU_BLOCKS = 8
LRU_BLOCK_W = 128
CONV_WIDTH = 4
LRU_C = 8.0
HEAD_DIM = 128
MOBA_BLOCK = 256
MOBA_TOPK = 3
MOBA_HEADS_PER_STEP = 4
MOBA_SWEEP = 4
ROPE_THETA = 500000.0
ROT_DIM = HEAD_DIM // 4
ROT_HALF = ROT_DIM // 2
NEG_INF = -1e30
N_EXPERTS = 8
LN_EPS = 1e-5
LANES = 128
SUBLANES = 8
MIB = 1024 * 1024


def _cparams(semantics, vmem_mib):
    return pltpu.CompilerParams(dimension_semantics=semantics, vmem_limit_bytes=vmem_mib * MIB)


def _resident(block_shape, index_map):
    return pl.BlockSpec(block_shape, index_map, pipeline_mode=pl.Buffered(1))


def _ada_kernel(c_ref, w_ref, b_ref, o_ref):
    c = c_ref[...]
    c_act = c * jax.nn.sigmoid(c)
    o_ref[0] = jnp.dot(c_act, w_ref[0], preferred_element_type=F32) + b_ref[0]


def _ada_all_layers(c_pad, w_ada, b_ada, tn=1024):
    depth, d, n = w_ada.shape
    rows = c_pad.shape[0]
    return pl.pallas_call(
        _ada_kernel,
        grid=(depth, n // tn),
        in_specs=[
            pl.BlockSpec((rows, d), lambda l, j: (0, 0)),
            pl.BlockSpec((1, d, tn), lambda l, j: (l, 0, j)),
            pl.BlockSpec((1, 1, tn), lambda l, j: (l, 0, j)),
        ],
        out_specs=pl.BlockSpec((1, rows, tn), lambda l, j: (l, 0, j)),
        out_shape=jax.ShapeDtypeStruct((depth, rows, n), F32),
        compiler_params=_cparams(("arbitrary", "arbitrary"), 40),
        name="ada",
    )(c_pad, w_ada, b_ada.reshape(depth, 1, n))


def _modulate_kernel(x_ref, ep_ref, h_ref):
    ep = ep_ref[0]
    h_ref[...] = (x_ref[...] * (1.0 + ep[1:2]) + ep[2:3]).astype(h_ref.dtype)


def _modulate(x2d, ep, seq, tm=512):
    t, d = x2d.shape
    per_b = seq // tm
    return pl.pallas_call(
        _modulate_kernel,
        grid=(t // tm,),
        in_specs=[
            pl.BlockSpec((tm, d), lambda i: (i, 0)),
            pl.BlockSpec((1, 3, d), lambda i: (i // per_b, 0, 0)),
        ],
        out_specs=pl.BlockSpec((tm, d), lambda i: (i, 0)),
        out_shape=jax.ShapeDtypeStruct((t, d), BF16),
        compiler_params=_cparams(("arbitrary",), 32),
        name="modulate",
    )(x2d, ep)


def _matmul_kernel(a_ref, w_ref, o_ref, wb_ref):
    @pl.when(pl.program_id(1) == 0)
    def _():
        wb_ref[...] = w_ref[...].astype(wb_ref.dtype)

    o_ref[...] = jnp.dot(a_ref[...], wb_ref[...], preferred_element_type=F32).astype(o_ref.dtype)


def _matmul(a, w, layer, tm=1024, tn=1024):
    m, k = a.shape
    n = w.shape[2]
    return pl.pallas_call(
        _matmul_kernel,
        grid=(n // tn, m // tm),
        in_specs=[
            pl.BlockSpec((tm, k), lambda j, i: (i, 0)),
            pl.BlockSpec((None, k, tn), lambda j, i: (layer, 0, j)),
        ],
        out_specs=pl.BlockSpec((tm, tn), lambda j, i: (i, j)),
        out_shape=jax.ShapeDtypeStruct((m, n), BF16),
        scratch_shapes=[pltpu.VMEM((k, tn), BF16)],
        compiler_params=_cparams(("arbitrary", "arbitrary"), 48),
        name="in_proj",
    )(a, w)


def _gelu_tanh(x):
    return 0.5 * x * (1.0 + jnp.tanh(0.7978845608028654 * (x + 0.044715 * (x * x * x))))


def _lru_kernel(x_ref, g_ref, cw_ref, cb_ref, wr_ref, br_ref, wi_ref, bi_ref, lam_ref,
                o_ref, tail_ref, carry_ref):
    @pl.when(pl.program_id(1) == 0)
    def _():
        tail_ref[...] = jnp.zeros_like(tail_ref)
        carry_ref[...] = jnp.zeros_like(carry_ref)

    ts, w = x_ref.shape
    groups = ts // SUBLANES
    x3 = x_ref[...].astype(F32).reshape(groups, SUBLANES, w)
    xe = jnp.concatenate([tail_ref[...][None], x3], axis=0)
    sub = lax.broadcasted_iota(jnp.int32, (groups, SUBLANES, w), 1)

    y = cb_ref[...][None] + cw_ref[CONV_WIDTH - 1:CONV_WIDTH, :][None] * x3
    for k in range(1, CONV_WIDTH):
        rolled = pltpu.roll(xe, k, axis=1)
        shifted = jnp.where(sub < k, rolled[:-1], rolled[1:])
        y = y + cw_ref[CONV_WIDTH - 1 - k:CONV_WIDTH - k, :][None] * shifted
    tail_ref[...] = x3[groups - 1]

    xa = y.reshape(ts, w)
    xab = xa.astype(BF16)

    def block_diag(w_ref, b_ref):
        parts = [
            jnp.dot(xab[:, n * LRU_BLOCK_W:(n + 1) * LRU_BLOCK_W], w_ref[n],
                    preferred_element_type=F32)
            for n in range(LRU_BLOCKS)
        ]
        return jnp.concatenate(parts, axis=1) + b_ref[...]

    r = jax.nn.sigmoid(block_diag(wr_ref, br_ref))
    i = jax.nn.sigmoid(block_diag(wi_ref, bi_ref))
    neg_lam = -lam_ref[...]
    softplus = jnp.maximum(neg_lam, 0.0) + jnp.log1p(jnp.exp(-jnp.abs(neg_lam)))
    a = jnp.exp((-LRU_C) * r * softplus)
    u = jnp.sqrt(1.0 - a * a) * (i * xa)

    a3 = a.reshape(groups, SUBLANES, w)
    u3 = u.reshape(groups, SUBLANES, w)
    d = 1
    while d < SUBLANES:
        a_prev = pltpu.roll(a3, d, axis=1)
        u_prev = pltpu.roll(u3, d, axis=1)
        take = sub >= d
        u3 = jnp.where(take, a3 * u_prev + u3, u3)
        a3 = jnp.where(take, a3 * a_prev, a3)
        d *= 2
    carry = carry_ref[...]
    hs = []
    for gi in range(groups):
        hg = u3[gi] + a3[gi] * carry
        hs.append(hg)
        carry = hg[SUBLANES - 1:SUBLANES, :]
    carry_ref[...] = carry
    h = jnp.stack(hs, axis=0).reshape(ts, w)

    o_ref[...] = (h * _gelu_tanh(g_ref[...].astype(F32))).astype(o_ref.dtype)


def _lru(proj, conv_w, conv_b, w_r, b_r, w_i, b_i, lam, batch, seq, ts=256):
    t = proj.shape[0]
    w = conv_w.shape[1]
    per_b = seq // ts
    row = lambda b, s: b * per_b + s
    vec = lambda v: v.reshape(1, w)
    const2 = lambda b, s: (0, 0)
    const3 = lambda b, s: (0, 0, 0)
    return pl.pallas_call(
        _lru_kernel,
        grid=(batch, per_b),
        in_specs=[
            pl.BlockSpec((ts, w), lambda b, s: (row(b, s), 0)),
            pl.BlockSpec((ts, w), lambda b, s: (row(b, s), 1)),
            pl.BlockSpec((CONV_WIDTH, w), const2),
            pl.BlockSpec((1, w), const2),
            pl.BlockSpec((LRU_BLOCKS, LRU_BLOCK_W, LRU_BLOCK_W), const3),
            pl.BlockSpec((1, w), const2),
            pl.BlockSpec((LRU_BLOCKS, LRU_BLOCK_W, LRU_BLOCK_W), const3),
            pl.BlockSpec((1, w), const2),
            pl.BlockSpec((1, w), const2),
        ],
        out_specs=pl.BlockSpec((ts, w), lambda b, s: (row(b, s), 0)),
        out_shape=jax.ShapeDtypeStruct((t, w), BF16),
        scratch_shapes=[pltpu.VMEM((SUBLANES, w), F32), pltpu.VMEM((1, w), F32)],
        compiler_params=_cparams(("arbitrary", "arbitrary"), 48),
        name="conv_rglru",
    )(proj, proj, conv_w, vec(conv_b), w_r.astype(BF16), vec(b_r), w_i.astype(BF16), vec(b_i),
      vec(lam))


def _rotary(x, cos_ref, sin_lo_ref, sin_hi_ref, rows):
    up = pltpu.roll(x, HEAD_DIM - ROT_HALF, axis=1)
    down = pltpu.roll(x, ROT_HALF, axis=1)
    return x * cos_ref[rows, :] + up * sin_lo_ref[rows, :] + down * sin_hi_ref[rows, :]


_NT = (((1,), (1,)), ((), ()))


def _moba_kernel(q_ref, k_ref, v_ref, cos_ref, sin_lo_ref, sin_hi_ref, o_ref,
                 krot_ref, vt_ref, vtg_ref, kmean_ref, sel_ref, raw_ref):
    qi = pl.program_id(2)
    n_heads, n_blk = kmean_ref.shape[0], kmean_ref.shape[1]
    scale2 = HEAD_DIM ** -0.5 * 1.4426950408889634
    head_cols = [slice(hh * HEAD_DIM, (hh + 1) * HEAD_DIM) for hh in range(n_heads)]

    @pl.when(qi == 0)
    def _():
        for hh in range(n_heads):
            for n in range(n_blk):
                rows = pl.ds(n * MOBA_BLOCK, MOBA_BLOCK)
                kr = _rotary(k_ref[rows, head_cols[hh]].astype(F32), cos_ref, sin_lo_ref,
                             sin_hi_ref, rows)
                krot_ref[hh, rows, :] = kr.astype(krot_ref.dtype)
                kmean_ref[hh, n:n + 1, :] = jnp.mean(kr, axis=0, keepdims=True)
                vt = v_ref[rows, head_cols[hh]].astype(F32).T.astype(vt_ref.dtype)
                vt_ref[hh, n] = vt
                c = n % MOBA_SWEEP
                vtg_ref[hh, n // MOBA_SWEEP, :, c * MOBA_BLOCK:(c + 1) * MOBA_BLOCK] = vt

    own_rows = pl.ds(pl.multiple_of(qi * MOBA_BLOCK, MOBA_BLOCK), MOBA_BLOCK)

    qts, init = [], []
    for hh in range(n_heads):
        qt = _rotary(q_ref[:, head_cols[hh]].astype(F32), cos_ref, sin_lo_ref, sin_hi_ref,
                     own_rows).T.astype(BF16)
        qts.append(qt)

        gate = jnp.dot(kmean_ref[hh].astype(BF16), qt, preferred_element_type=F32)
        blk_i = lax.broadcasted_iota(jnp.int32, gate.shape, 0)
        blk = blk_i.astype(F32)
        past = blk_i < qi
        g = jnp.where(past, gate, -jnp.inf)
        sel = jnp.zeros_like(gate)
        for _ in range(MOBA_TOPK):
            best = jnp.max(g, axis=0, keepdims=True)
            idx = jnp.min(jnp.where(g == best, blk, float(n_blk)), axis=0, keepdims=True)
            pick = blk == idx
            sel = jnp.where(pick, 1.0, sel)
            g = jnp.where(pick, -jnp.inf, g)
        sel_ref[hh] = jnp.where(past, sel, 0.0)

        s = jnp.dot(krot_ref[hh, own_rows, :], qt, preferred_element_type=F32) * scale2
        key_id = lax.broadcasted_iota(jnp.int32, s.shape, 0)
        qry_id = lax.broadcasted_iota(jnp.int32, s.shape, 1)
        s = jnp.where(key_id <= qry_id, s, NEG_INF)
        m0 = jnp.max(s, axis=0, keepdims=True)
        p = jnp.exp2(s - m0)
        l0 = jnp.sum(p, axis=0, keepdims=True)
        acc0 = jnp.dot(vt_ref[hh, qi], p.astype(BF16), preferred_element_type=F32)
        init.append((m0, l0, acc0))

    span = MOBA_SWEEP * MOBA_BLOCK
    last_group = n_blk // MOBA_SWEEP - 1

    def raw_scores(hh, j):
        rows = pl.ds(pl.multiple_of(j * span, span), span)
        return jnp.dot(krot_ref[hh, rows, :], qts[hh], preferred_element_type=F32)

    for hh in range(n_heads):
        raw_ref[hh] = raw_scores(hh, 0)

    def past_group(j, carry):
        out = []
        for hh in range(n_heads):
            m, l, acc = carry[hh]
            s = jnp.concatenate([
                jnp.where(sel_ref[hh, pl.ds(j * MOBA_SWEEP + c, 1), :] > 0.0,
                          raw_ref[hh, c * MOBA_BLOCK:(c + 1) * MOBA_BLOCK, :] * scale2, NEG_INF)
                for c in range(MOBA_SWEEP)], axis=0)
            raw_ref[hh] = raw_scores(hh, jnp.minimum(j + 1, last_group))
            m_new = jnp.maximum(m, jnp.max(s, axis=0, keepdims=True))
            alpha = jnp.exp2(m - m_new)
            p = jnp.exp2(s - m_new)
            l = alpha * l + jnp.sum(p, axis=0, keepdims=True)
            acc = alpha * acc + jnp.dot(vtg_ref[hh, j], p.astype(BF16), preferred_element_type=F32)
            out.append((m_new, l, acc))
        return tuple(out)

    n_groups = lax.div(qi + (MOBA_SWEEP - 1), MOBA_SWEEP)
    final = lax.fori_loop(0, n_groups, past_group, tuple(init))
    for hh in range(n_heads):
        _, l, acc = final[hh]
        o_ref[:, head_cols[hh]] = (acc / l).T.astype(o_ref.dtype)


def _rope_tables(seq):
    pos = jnp.arange(seq, dtype=F32)
    inv_freq = ROPE_THETA ** (-jnp.arange(0, ROT_DIM, 2, dtype=F32) / ROT_DIM)
    ang = pos[:, None] * inv_freq[None, :]
    cos, sin = jnp.cos(ang), jnp.sin(ang)
    rest = HEAD_DIM - ROT_DIM
    cos_t = jnp.concatenate([cos, cos, jnp.ones((seq, rest), F32)], axis=1)
    sin_lo = jnp.concatenate([-sin, jnp.zeros((seq, HEAD_DIM - ROT_HALF), F32)], axis=1)
    sin_hi = jnp.concatenate([jnp.zeros((seq, ROT_HALF), F32), sin, jnp.zeros((seq, rest), F32)], axis=1)
    return cos_t, sin_lo, sin_hi


def _moba(proj, tables, batch, seq, heads, q_col, k_col, v_col, hg=MOBA_HEADS_PER_STEP):
    t = proj.shape[0]
    n_blk = seq // MOBA_BLOCK
    gw = hg * HEAD_DIM
    table_spec = _resident((seq, HEAD_DIM), lambda b, h, qi: (0, 0))
    return pl.pallas_call(
        _moba_kernel,
        grid=(batch, heads // hg, n_blk),
        in_specs=[
            pl.BlockSpec((MOBA_BLOCK, gw), lambda b, h, qi: (b * n_blk + qi, q_col + h)),
            pl.BlockSpec((seq, gw), lambda b, h, qi: (b, k_col + h)),
            pl.BlockSpec((seq, gw), lambda b, h, qi: (b, v_col + h)),
            table_spec, table_spec, table_spec,
        ],
        out_specs=pl.BlockSpec((MOBA_BLOCK, gw), lambda b, h, qi: (b * n_blk + qi, h)),
        out_shape=jax.ShapeDtypeStruct((t, heads * HEAD_DIM), BF16),
        scratch_shapes=[
            pltpu.VMEM((hg, seq, HEAD_DIM), BF16),
            pltpu.VMEM((hg, n_blk, HEAD_DIM, MOBA_BLOCK), BF16),
            pltpu.VMEM((hg, n_blk // MOBA_SWEEP, HEAD_DIM, MOBA_SWEEP * MOBA_BLOCK), BF16),
            pltpu.VMEM((hg, n_blk, HEAD_DIM), F32),
            pltpu.VMEM((hg, n_blk, MOBA_BLOCK), F32),
            pltpu.VMEM((hg, MOBA_SWEEP * MOBA_BLOCK, MOBA_BLOCK), F32),
        ],
        compiler_params=_cparams(("arbitrary", "arbitrary", "arbitrary"), 48),
        name="moba_attention",
    )(proj, proj, proj, *tables)


def _deepnorm_ln(x, y, ep, ln_g, ln_b, alpha):
    r = alpha * x + (1.0 + ep[0:1]) * y
    mu = jnp.mean(r, axis=-1, keepdims=True)
    rc = r - mu
    var = jnp.mean(rc * rc, axis=-1, keepdims=True)
    return rc * lax.rsqrt(var + LN_EPS) * ln_g + ln_b


def _next_h(x_new, ep):
    return x_new * (1.0 + ep[1:2]) + ep[2:3]


def _mix_kernel(alpha, ya_ref, at_ref, ga0_ref, ga1_ref, gb0_ref, gb1_ref, x_ref, ep_ref,
                wpa_ref, wpb_ref, wout_ref, lng_ref, lnb_ref, xo_ref, ho_ref):
    y_a = jnp.dot(ya_ref[...], wpa_ref[...], preferred_element_type=F32)
    y_b = jnp.dot(at_ref[...], wpb_ref[...], preferred_element_type=F32)
    gate_a = jnp.concatenate([ga0_ref[...], ga1_ref[...]], axis=1).astype(F32)
    gate_b = jnp.concatenate([gb0_ref[...], gb1_ref[...]], axis=1).astype(F32)
    merged = jax.nn.sigmoid(gate_a) * y_a + jax.nn.sigmoid(gate_b) * y_b
    y = jnp.dot(merged.astype(BF16), wout_ref[...], preferred_element_type=F32)
    ep = ep_ref[0]
    x_new = _deepnorm_ln(x_ref[...], y, ep, lng_ref[...], lnb_ref[...], alpha)
    xo_ref[...] = x_new
    ho_ref[...] = _next_h(x_new, ep).astype(ho_ref.dtype)


def _mix(ya, attn, proj, x2d, ep, w_pa, w_pb, w_out, ln_g, ln_b, alpha, seq, ga_col, gb_col, tm=256):
    t, d = x2d.shape
    wa = ya.shape[1]
    half = d // 2
    per_b = seq // tm
    const = lambda i: (0, 0)
    return pl.pallas_call(
        functools.partial(_mix_kernel, alpha),
        grid=(t // tm,),
        in_specs=[
            pl.BlockSpec((tm, wa), lambda i: (i, 0)),
            pl.BlockSpec((tm, wa), lambda i: (i, 0)),
            pl.BlockSpec((tm, half), lambda i: (i, ga_col)),
            pl.BlockSpec((tm, half), lambda i: (i, ga_col + 1)),
            pl.BlockSpec((tm, half), lambda i: (i, gb_col)),
            pl.BlockSpec((tm, half), lambda i: (i, gb_col + 1)),
            pl.BlockSpec((tm, d), lambda i: (i, 0)),
            pl.BlockSpec((1, 3, d), lambda i: (i // per_b, 0, 0)),
            _resident((wa, d), const),
            _resident((wa, d), const),
            _resident((d, d), const),
            _resident((1, d), const),
            _resident((1, d), const),
        ],
        out_specs=[pl.BlockSpec((tm, d), lambda i: (i, 0)), pl.BlockSpec((tm, d), lambda i: (i, 0))],
        out_shape=[jax.ShapeDtypeStruct((t, d), F32), jax.ShapeDtypeStruct((t, d), BF16)],
        compiler_params=_cparams(("arbitrary",), 56),
        name="mix_out_ln",
    )(ya, attn, proj, proj, proj, proj, x2d, ep, w_pa, w_pb, w_out, ln_g.reshape(1, d),
      ln_b.reshape(1, d))


def _swiglu_step(h, wg, wu, wd):
    g = jnp.dot(h, wg, preferred_element_type=F32)
    u = jnp.dot(h, wu, preferred_element_type=F32)
    return jnp.dot((g * jax.nn.sigmoid(g) * u).astype(BF16), wd, preferred_element_type=F32)


def _ffn_kernel(alpha, h_ref, wg_ref, wu_ref, wd_ref, x_ref, ep_ref, lng_ref, lnb_ref,
                xo_ref, ho_ref, acc_ref):
    f = pl.program_id(1)

    @pl.when(f == 0)
    def _():
        acc_ref[...] = jnp.zeros_like(acc_ref)

    acc_ref[...] += _swiglu_step(h_ref[...], wg_ref[...], wu_ref[...], wd_ref[...])

    @pl.when(f == pl.num_programs(1) - 1)
    def _():
        ep = ep_ref[0]
        x_new = _deepnorm_ln(x_ref[...], acc_ref[...], ep, lng_ref[...], lnb_ref[...], alpha)
        xo_ref[...] = x_new
        ho_ref[...] = _next_h(x_new, ep).astype(ho_ref.dtype)


def _ffn(h, x2d, ep, w_gate, w_up, w_down, ln_g, ln_b, alpha, seq, tm=512, tf=512):
    t, d = x2d.shape
    ff = w_gate.shape[1]
    per_b = seq // tm
    return pl.pallas_call(
        functools.partial(_ffn_kernel, alpha),
        grid=(t // tm, ff // tf),
        in_specs=[
            pl.BlockSpec((tm, d), lambda i, f: (i, 0)),
            pl.BlockSpec((d, tf), lambda i, f: (0, f)),
            pl.BlockSpec((d, tf), lambda i, f: (0, f)),
            pl.BlockSpec((tf, d), lambda i, f: (f, 0)),
            pl.BlockSpec((tm, d), lambda i, f: (i, 0)),
            pl.BlockSpec((1, 3, d), lambda i, f: (i // per_b, 0, 0)),
            pl.BlockSpec((1, d), lambda i, f: (0, 0)),
            pl.BlockSpec((1, d), lambda i, f: (0, 0)),
        ],
        out_specs=[pl.BlockSpec((tm, d), lambda i, f: (i, 0)), pl.BlockSpec((tm, d), lambda i, f: (i, 0))],
        out_shape=[jax.ShapeDtypeStruct((t, d), F32), jax.ShapeDtypeStruct((t, d), BF16)],
        scratch_shapes=[pltpu.VMEM((tm, d), F32)],
        compiler_params=_cparams(("arbitrary", "arbitrary"), 56),
        name="ffn_dense_ln",
    )(h, w_gate, w_up, w_down, x2d, ep, ln_g.reshape(1, d), ln_b.reshape(1, d))


def _router_kernel(h_ref, w_ref, b_ref, o_ref, hp_ref):
    logits = jnp.dot(h_ref[...], w_ref[...], preferred_element_type=F32) + b_ref[...]
    lane_i = lax.broadcasted_iota(jnp.int32, logits.shape, 1)
    lane = lane_i.astype(F32)
    lg = jnp.where(lane_i < N_EXPERTS, logits, -jnp.inf)
    m1 = jnp.max(lg, axis=1, keepdims=True)
    i1 = jnp.min(jnp.where(lg == m1, lane, float(LANES)), axis=1, keepdims=True)
    lg2 = jnp.where(lane == i1, -jnp.inf, lg)
    m2 = jnp.max(lg2, axis=1, keepdims=True)
    i2 = jnp.min(jnp.where(lg2 == m2, lane, float(LANES)), axis=1, keepdims=True)
    e2 = jnp.exp(m2 - m1)
    denom = 1.0 + e2
    out = jnp.where(lane_i == 0, i1, 0.0)
    out = jnp.where(lane_i == 1, i2, out)
    out = jnp.where(lane_i == 2, 1.0 / denom, out)
    out = jnp.where(lane_i == 3, e2 / denom, out)
    o_ref[...] = out
    _store_token_tiles(hp_ref, _pack_bf16_pairs(h_ref[...]))


def _pack_bf16_pairs(h):
    half = h.shape[1] // 2
    bits = pltpu.bitcast(h.astype(F32), jnp.uint32)
    return bits[:, :half] | lax.shift_right_logical(bits[:, half:], jnp.uint32(16))


def _store_token_tiles(ref, rows):
    m = rows.shape[0]
    n = rows.shape[1] // LANES
    for c in range(n):
        ref[pl.ds(c, m, stride=n), :] = rows[:, c * LANES:(c + 1) * LANES]


def _load_token_tiles(ref, n):
    m = ref.shape[0] // n
    return jnp.concatenate([ref[pl.ds(c, m, stride=n), :] for c in range(n)], axis=1)


def _unpack_bf16_pairs(packed):
    hi = pltpu.bitcast(packed & jnp.uint32(0xFFFF0000), F32)
    lo = pltpu.bitcast(lax.shift_left(packed, jnp.uint32(16)), F32)
    return jnp.concatenate([hi, lo], axis=1).astype(BF16)


def _router(h, w_router_pad, b_router_pad, tm=1024):
    t, d = h.shape
    return pl.pallas_call(
        _router_kernel,
        grid=(t // tm,),
        in_specs=[
            pl.BlockSpec((tm, d), lambda i: (i, 0)),
            pl.BlockSpec((d, LANES), lambda i: (0, 0)),
            pl.BlockSpec((1, LANES), lambda i: (0, 0)),
        ],
        out_specs=[pl.BlockSpec((tm, LANES), lambda i: (i, 0)),
                   pl.BlockSpec((tm * (d // 2 // LANES), LANES), lambda i: (i, 0))],
        out_shape=[jax.ShapeDtypeStruct((t, LANES), F32),
                   jax.ShapeDtypeStruct((t * (d // 2 // LANES), LANES), jnp.uint32)],
        compiler_params=_cparams(("arbitrary",), 32),
        name="moe_router",
    )(h, w_router_pad, b_router_pad)


def _gmm_kernel(step_rows, te_ref, tv_ref, src_ref, dst_ref, hp_ref, wg_ref, wu_ref, wd_ref, y_ref,
                xbuf_ref, x_ref, acc_ref, ybuf_ref, gather_sem, scatter_sem):
    i = pl.program_id(0)
    f = pl.program_id(1)
    n_tiles = pl.num_programs(0)
    n_f = pl.num_programs(1)
    tm = acc_ref.shape[0]
    n_in = xbuf_ref.shape[1] // tm
    n_out = ybuf_ref.shape[0] // tm
    valid = tv_ref[i] > 0
    slot = lax.rem(i, 2)
    prev_valid = jnp.logical_and(i > 0, tv_ref[jnp.maximum(i - 1, 0)] > 0)
    nxt = jnp.minimum(i + 1, n_tiles - 1)
    next_valid = jnp.logical_and(i + 1 < n_tiles, tv_ref[nxt] > 0)

    lo = jnp.minimum(f * step_rows, tm)
    hi = jnp.minimum(lo + step_rows, tm)

    def tile_rows(row, n):
        return pl.ds(pl.multiple_of(row * n, n), n)

    def gather_rows(tile, to_slot, first, stop):
        base = tile * tm

        def row(r, carry):
            pltpu.make_async_copy(hp_ref.at[tile_rows(src_ref[base + r], n_in), :],
                                  xbuf_ref.at[to_slot, tile_rows(r, n_in), :],
                                  gather_sem.at[to_slot]).start()
            return carry

        lax.fori_loop(first, stop, row, 0)

    def gather_wait(from_slot):
        pltpu.make_async_copy(hp_ref.at[pl.ds(0, tm * n_in), :], xbuf_ref.at[from_slot],
                              gather_sem.at[from_slot]).wait()

    def scatter_rows(tile, first, stop):
        base = tile * tm

        def row(r, carry):
            pltpu.make_async_copy(ybuf_ref.at[tile_rows(r, n_out), :],
                                  y_ref.at[tile_rows(dst_ref[base + r], n_out), :],
                                  scatter_sem.at[0]).start()
            return carry

        lax.fori_loop(first, stop, row, 0)

    def scatter_wait():
        pltpu.make_async_copy(ybuf_ref, y_ref.at[pl.ds(0, tm * n_out), :],
                              scatter_sem.at[0]).wait()

    @pl.when(jnp.logical_and(f == 0, i == 0))
    def _():
        gather_rows(0, 0, 0, tm)

    @pl.when(valid)
    def _():
        @pl.when(f == 0)
        def _():
            gather_wait(slot)
            x_ref[...] = _unpack_bf16_pairs(_load_token_tiles(xbuf_ref.at[slot], n_in))

        @pl.when(next_valid)
        def _():
            gather_rows(nxt, 1 - slot, lo, hi)

        @pl.when(prev_valid)
        def _():
            scatter_rows(i - 1, lo, hi)

        res = _swiglu_step(x_ref[...], wg_ref[...].astype(BF16), wu_ref[...].astype(BF16),
                           wd_ref[...].astype(BF16))

        @pl.when(f == 0)
        def _():
            acc_ref[...] = res

        @pl.when(f > 0)
        def _():
            acc_ref[...] += res

        @pl.when(f == n_f - 1)
        def _():
            @pl.when(prev_valid)
            def _():
                scatter_wait()

            _store_token_tiles(ybuf_ref, acc_ref[...])

            @pl.when(jnp.logical_not(next_valid))
            def _():
                scatter_rows(i, 0, tm)
                scatter_wait()


def _gmm(tile_expert, tile_valid, src_token, dst_row, hp, w_gate, w_up, w_down, layer, n_out_rows,
         tm, tf=256):
    r = src_token.shape[0]
    d = w_gate.shape[2]
    ff = w_gate.shape[3]
    nf = ff // tf

    def f_idx(i, f, tv):
        return jnp.where(tv[i] > 0, f, nf - 1)

    grid_spec = pltpu.PrefetchScalarGridSpec(
        num_scalar_prefetch=4,
        grid=(r // tm, nf),
        in_specs=[
            pl.BlockSpec(memory_space=pl.ANY),
            pl.BlockSpec((None, None, d, tf),
                         lambda i, f, te, tv, src, dst: (layer, te[i], 0, f_idx(i, f, tv))),
            pl.BlockSpec((None, None, d, tf),
                         lambda i, f, te, tv, src, dst: (layer, te[i], 0, f_idx(i, f, tv))),
            pl.BlockSpec((None, None, tf, d),
                         lambda i, f, te, tv, src, dst: (layer, te[i], f_idx(i, f, tv), 0)),
        ],
        out_specs=pl.BlockSpec(memory_space=pl.ANY),
        scratch_shapes=[
            pltpu.VMEM((2, tm * (d // 2 // LANES), LANES), jnp.uint32),
            pltpu.VMEM((tm, d), BF16),
            pltpu.VMEM((tm, d), F32),
            pltpu.VMEM((tm * (d // LANES), LANES), F32),
            pltpu.SemaphoreType.DMA((2,)),
            pltpu.SemaphoreType.DMA((1,)),
        ],
    )
    step_rows = pl.cdiv(pl.cdiv(tm, nf), SUBLANES) * SUBLANES
    return pl.pallas_call(
        functools.partial(_gmm_kernel, step_rows),
        grid_spec=grid_spec,
        out_shape=jax.ShapeDtypeStruct((n_out_rows * (d // LANES), LANES), F32),
        compiler_params=pltpu.CompilerParams(
            dimension_semantics=("arbitrary", "arbitrary"), vmem_limit_bytes=56 * MIB,
            disable_bounds_checks=True),
        name="moe_grouped_swiglu",
    )(tile_expert, tile_valid, src_token, dst_row, hp, w_gate, w_up, w_down)


def _combine_ln_kernel(alpha, route_ref, y1_ref, y2_ref, x_ref, ep_ref, lng_ref, lnb_ref,
                       xo_ref, ho_ref):
    ep = ep_ref[0]
    route = route_ref[...]
    n = x_ref.shape[1] // LANES
    y = (route[:, 2:3] * _load_token_tiles(y1_ref, n)
         + route[:, 3:4] * _load_token_tiles(y2_ref, n))
    x_new = _deepnorm_ln(x_ref[...], y, ep, lng_ref[...], lnb_ref[...], alpha)
    xo_ref[...] = x_new
    ho_ref[...] = _next_h(x_new, ep).astype(ho_ref.dtype)


def _combine_ln(route, y_slots, x2d, ep, ln_g, ln_b, alpha, seq, tm=256):
    t, d = x2d.shape
    per_b = seq // tm
    n_tiles = t // tm
    return pl.pallas_call(
        functools.partial(_combine_ln_kernel, alpha),
        grid=(n_tiles,),
        in_specs=[
            pl.BlockSpec((tm, LANES), lambda i: (i, 0)),
            pl.BlockSpec((tm * (d // LANES), LANES), lambda i: (i, 0)),
            pl.BlockSpec((tm * (d // LANES), LANES), lambda i: (i + n_tiles, 0)),
            pl.BlockSpec((tm, d), lambda i: (i, 0)),
            pl.BlockSpec((1, 3, d), lambda i: (i // per_b, 0, 0)),
            pl.BlockSpec((1, d), lambda i: (0, 0)),
            pl.BlockSpec((1, d), lambda i: (0, 0)),
        ],
        out_specs=[pl.BlockSpec((tm, d), lambda i: (i, 0)), pl.BlockSpec((tm, d), lambda i: (i, 0))],
        out_shape=[jax.ShapeDtypeStruct((t, d), F32), jax.ShapeDtypeStruct((t, d), BF16)],
        compiler_params=_cparams(("arbitrary",), 48),
        name="moe_combine_ln",
    )(route, y_slots, y_slots, x2d, ep, ln_g.reshape(1, d), ln_b.reshape(1, d))


def _dispatch_plan(route, tm):
    t = route.shape[0]
    n_pairs = 2 * t
    n_rows = n_pairs + N_EXPERTS * tm
    expert = route[:, :2].astype(jnp.int32).reshape(n_pairs)
    onehot = (expert[:, None] == jnp.arange(N_EXPERTS, dtype=jnp.int32)[None, :]).astype(jnp.int32)
    rank = jnp.take_along_axis(jnp.cumsum(onehot, axis=0) - onehot, expert[:, None], axis=1)[:, 0]
    counts = jnp.sum(onehot, axis=0)
    padded = ((counts + tm - 1) // tm) * tm
    ends = jnp.cumsum(padded)
    starts = ends - padded
    pos = starts[expert] + rank
    pair = jnp.full((n_rows,), -1, jnp.int32).at[pos].set(jnp.arange(n_pairs, dtype=jnp.int32))
    token = jnp.maximum(pair, 0) // 2
    is_pad = (pair < 0).astype(jnp.int32)
    src_token = jnp.where(pair >= 0, token, 0)
    dst_row = jnp.where(pair >= 0, (pair % 2) * t + token, n_pairs + jnp.cumsum(is_pad) - 1)
    tile_start = jnp.arange(n_rows // tm, dtype=jnp.int32) * tm
    tile_valid = (tile_start < ends[-1]).astype(jnp.int32)
    n_ended = jnp.sum((tile_start[:, None] >= ends[None, :]).astype(jnp.int32), axis=1)
    tile_expert = jnp.minimum(n_ended, N_EXPERTS - 1)
    last_valid_expert = jnp.max(jnp.where(tile_valid > 0, tile_expert, 0))
    tile_expert = jnp.where(tile_valid > 0, tile_expert, last_valid_expert)
    return src_token, dst_row, tile_expert, tile_valid


def kernel(x, c, w_in, conv_w, conv_b, w_rgate, b_rgate, w_igate, b_igate, lru_lambda, w_proj_a, w_proj_b, w_out, w_ada, b_ada, ln_mix_g, ln_mix_b, ln_ffn_g, ln_ffn_b, ffn_w_gate, ffn_w_up, ffn_w_down, moe_w_router, moe_b_router, moe_w_gate, moe_w_up, moe_w_down):
    batch, seq, d = x.shape
    depth = w_in.shape[0]
    t = batch * seq
    lru_w = conv_w.shape[2]
    attn_w = w_proj_b.shape[1]
    heads = attn_w // HEAD_DIM
    alpha = (2.0 * depth) ** 0.25
    moe_tm = 1024

    c_pad = jnp.zeros((SUBLANES, d), F32).at[:batch].set(c)
    ada = _ada_all_layers(c_pad, w_ada, b_ada)[:, :batch].reshape(depth, batch, 6, d)
    zeros_bd = jnp.zeros((batch, d), F32)

    def epilogue_params(gate, scale, shift):
        return jnp.stack([gate, scale, shift], axis=1)

    tables = _rope_tables(seq)
    x2d = x.reshape(t, d)
    h = _modulate(x2d, epilogue_params(zeros_bd, ada[0, :, 1], ada[0, :, 0]), seq)

    head_group_w = MOBA_HEADS_PER_STEP * HEAD_DIM
    q_col = 2 * lru_w // head_group_w
    k_col = q_col + attn_w // head_group_w
    v_col = k_col + attn_w // head_group_w
    ga_col = (2 * lru_w + 3 * attn_w) // (d // 2)
    gb_col = ga_col + 2

    for l in range(depth):
        sh1, sc1, g1, sh2, sc2, g2 = (ada[l, :, j] for j in range(6))
        proj = _matmul(h, w_in, l)
        ya = _lru(proj, conv_w[l], conv_b[l], w_rgate[l], b_rgate[l], w_igate[l], b_igate[l],
                  lru_lambda[l], batch, seq)
        attn = _moba(proj, tables, batch, seq, heads, q_col, k_col, v_col)
        x2d, h = _mix(ya, attn, proj, x2d, epilogue_params(g1, sc2, sh2),
                      w_proj_a[l].astype(BF16), w_proj_b[l].astype(BF16), w_out[l].astype(BF16),
                      ln_mix_g[l], ln_mix_b[l], alpha, seq, ga_col, gb_col)
        if l + 1 < depth:
            ep = epilogue_params(g2, ada[l + 1, :, 1], ada[l + 1, :, 0])
        else:
            ep = epilogue_params(g2, zeros_bd, zeros_bd)
        j = l // 2
        if l % 2 == 0:
            x2d, h = _ffn(h, x2d, ep, ffn_w_gate[j].astype(BF16), ffn_w_up[j].astype(BF16),
                          ffn_w_down[j].astype(BF16), ln_ffn_g[l], ln_ffn_b[l], alpha, seq)
        else:
            w_r_pad = jnp.zeros((d, LANES), BF16).at[:, :N_EXPERTS].set(moe_w_router[j].astype(BF16))
            b_r_pad = jnp.zeros((1, LANES), F32).at[0, :N_EXPERTS].set(moe_b_router[j])
            route, h_packed = _router(h, w_r_pad, b_r_pad)
            src_token, dst_row, tile_expert, tile_valid = _dispatch_plan(route, moe_tm)
            y_slots = _gmm(tile_expert, tile_valid, src_token, dst_row, h_packed, moe_w_gate,
                           moe_w_up, moe_w_down, j, src_token.shape[0], moe_tm)
            x2d, h = _combine_ln(route, y_slots, x2d, ep, ln_ffn_g[l], ln_ffn_b[l], alpha, seq)
    return x2d.reshape(batch, seq, d)
```

```python
import functools

import jax
import jax.numpy as jnp
from jax import lax
from jax.experimental import pallas as pl
from jax.experimental.pallas import tpu as pltpu

F32 = jnp.float32
BF16 = jnp.bfloat16

LRU_BLOCKS = 8
LRU_BLOCK_W = 128
CONV_WIDTH = 4
LRU_C = 8.0
HEAD_DIM = 128
MOBA_BLOCK = 256
MOBA_TOPK = 3
MOBA_HEADS_PER_STEP = 4
MOBA_SWEEP = 4
ROPE_THETA = 500000.0
ROT_DIM = HEAD_DIM // 4
ROT_HALF = ROT_DIM // 2
NEG_INF = -1e30
N_EXPERTS = 8
LN_EPS = 1e-5
LANES = 128
SUBLANES = 8
MIB = 1024 * 1024


def _cparams(semantics, vmem_mib):
    return pltpu.CompilerParams(dimension_semantics=semantics, vmem_limit_bytes=vmem_mib * MIB)


def _resident(block_shape, index_map):
    return pl.BlockSpec(block_shape, index_map, pipeline_mode=pl.Buffered(1))


def _ada_kernel(c_ref, w_ref, b_ref, o_ref):
    c = c_ref[...]
    c_act = c * jax.nn.sigmoid(c)
    o_ref[0] = jnp.dot(c_act, w_ref[0], preferred_element_type=F32) + b_ref[0]


def _ada_all_layers(c_pad, w_ada, b_ada, tn=1024):
    depth, d, n = w_ada.shape
    rows = c_pad.shape[0]
    return pl.pallas_call(
        _ada_kernel,
        grid=(depth, n // tn),
        in_specs=[
            pl.BlockSpec((rows, d), lambda l, j: (0, 0)),
            pl.BlockSpec((1, d, tn), lambda l, j: (l, 0, j)),
            pl.BlockSpec((1, 1, tn), lambda l, j: (l, 0, j)),
        ],
        out_specs=pl.BlockSpec((1, rows, tn), lambda l, j: (l, 0, j)),
        out_shape=jax.ShapeDtypeStruct((depth, rows, n), F32),
        compiler_params=_cparams(("arbitrary", "arbitrary"), 40),
        name="ada",
    )(c_pad, w_ada, b_ada.reshape(depth, 1, n))


def _modulate_kernel(x_ref, ep_ref, h_ref):
    ep = ep_ref[0]
    h_ref[...] = (x_ref[...] * (1.0 + ep[1:2]) + ep[2:3]).astype(h_ref.dtype)


def _modulate(x2d, ep, seq, tm=512):
    t, d = x2d.shape
    per_b = seq // tm
    return pl.pallas_call(
        _modulate_kernel,
        grid=(t // tm,),
        in_specs=[
            pl.BlockSpec((tm, d), lambda i: (i, 0)),
            pl.BlockSpec((1, 3, d), lambda i: (i // per_b, 0, 0)),
        ],
        out_specs=pl.BlockSpec((tm, d), lambda i: (i, 0)),
        out_shape=jax.ShapeDtypeStruct((t, d), BF16),
        compiler_params=_cparams(("arbitrary",), 32),
        name="modulate",
    )(x2d, ep)


def _matmul_kernel(a_ref, w_ref, o_ref, wb_ref):
    @pl.when(pl.program_id(1) == 0)
    def _():
        wb_ref[...] = w_ref[...].astype(wb_ref.dtype)

    o_ref[...] = jnp.dot(a_ref[...], wb_ref[...], preferred_element_type=F32).astype(o_ref.dtype)


def _matmul(a, w, layer, tm=1024, tn=1024):
    m, k = a.shape
    n = w.shape[2]
    return pl.pallas_call(
        _matmul_kernel,
        grid=(n // tn, m // tm),
        in_specs=[
            pl.BlockSpec((tm, k), lambda j, i: (i, 0)),
            pl.BlockSpec((None, k, tn), lambda j, i: (layer, 0, j)),
        ],
        out_specs=pl.BlockSpec((tm, tn), lambda j, i: (i, j)),
        out_shape=jax.ShapeDtypeStruct((m, n), BF16),
        scratch_shapes=[pltpu.VMEM((k, tn), BF16)],
        compiler_params=_cparams(("arbitrary", "arbitrary"), 48),
        name="in_proj",
    )(a, w)


def _gelu_tanh(x):
    return 0.5 * x * (1.0 + jnp.tanh(0.7978845608028654 * (x + 0.044715 * (x * x * x))))


def _lru_kernel(x_ref, g_ref, cw_ref, cb_ref, wr_ref, br_ref, wi_ref, bi_ref, lam_ref,
                o_ref, tail_ref, carry_ref):
    @pl.when(pl.program_id(1) == 0)
    def _():
        tail_ref[...] = jnp.zeros_like(tail_ref)
        carry_ref[...] = jnp.zeros_like(carry_ref)

    ts, w = x_ref.shape
    groups = ts // SUBLANES
    x3 = x_ref[...].astype(F32).reshape(groups, SUBLANES, w)
    xe = jnp.concatenate([tail_ref[...][None], x3], axis=0)
    sub = lax.broadcasted_iota(jnp.int32, (groups, SUBLANES, w), 1)

    y = cb_ref[...][None] + cw_ref[CONV_WIDTH - 1:CONV_WIDTH, :][None] * x3
    for k in range(1, CONV_WIDTH):
        rolled = pltpu.roll(xe, k, axis=1)
        shifted = jnp.where(sub < k, rolled[:-1], rolled[1:])
        y = y + cw_ref[CONV_WIDTH - 1 - k:CONV_WIDTH - k, :][None] * shifted
    tail_ref[...] = x3[groups - 1]

    xa = y.reshape(ts, w)
    xab = xa.astype(BF16)

    def block_diag(w_ref, b_ref):
        parts = [
            jnp.dot(xab[:, n * LRU_BLOCK_W:(n + 1) * LRU_BLOCK_W], w_ref[n],
                    preferred_element_type=F32)
            for n in range(LRU_BLOCKS)
        ]
        return jnp.concatenate(parts, axis=1) + b_ref[...]

    r = jax.nn.sigmoid(block_diag(wr_ref, br_ref))
    i = jax.nn.sigmoid(block_diag(wi_ref, bi_ref))
    neg_lam = -lam_ref[...]
    softplus = jnp.maximum(neg_lam, 0.0) + jnp.log1p(jnp.exp(-jnp.abs(neg_lam)))
    a = jnp.exp((-LRU_C) * r * softplus)
    u = jnp.sqrt(1.0 - a * a) * (i * xa)

    a3 = a.reshape(groups, SUBLANES, w)
    u3 = u.reshape(groups, SUBLANES, w)
    d = 1
    while d < SUBLANES:
        a_prev = pltpu.roll(a3, d, axis=1)
        u_prev = pltpu.roll(u3, d, axis=1)
        take = sub >= d
        u3 = jnp.where(take, a3 * u_prev + u3, u3)
        a3 = jnp.where(take, a3 * a_prev, a3)
        d *= 2
    carry = carry_ref[...]
    hs = []
    for gi in range(groups):
        hg = u3[gi] + a3[gi] * carry
        hs.append(hg)
        carry = hg[SUBLANES - 1:SUBLANES, :]
    carry_ref[...] = carry
    h = jnp.stack(hs, axis=0).reshape(ts, w)

    o_ref[...] = (h * _gelu_tanh(g_ref[...].astype(F32))).astype(o_ref.dtype)


def _lru(proj, conv_w, conv_b, w_r, b_r, w_i, b_i, lam, batch, seq, ts=256):
    t = proj.shape[0]
    w = conv_w.shape[1]
    per_b = seq // ts
    row = lambda b, s: b * per_b + s
    vec = lambda v: v.reshape(1, w)
    const2 = lambda b, s: (0, 0)
    const3 = lambda b, s: (0, 0, 0)
    return pl.pallas_call(
        _lru_kernel,
        grid=(batch, per_b),
        in_specs=[
            pl.BlockSpec((ts, w), lambda b, s: (row(b, s), 0)),
            pl.BlockSpec((ts, w), lambda b, s: (row(b, s), 1)),
            pl.BlockSpec((CONV_WIDTH, w), const2),
            pl.BlockSpec((1, w), const2),
            pl.BlockSpec((LRU_BLOCKS, LRU_BLOCK_W, LRU_BLOCK_W), const3),
            pl.BlockSpec((1, w), const2),
            pl.BlockSpec((LRU_BLOCKS, LRU_BLOCK_W, LRU_BLOCK_W), const3),
            pl.BlockSpec((1, w), const2),
            pl.BlockSpec((1, w), const2),
        ],
        out_specs=pl.BlockSpec((ts, w), lambda b, s: (row(b, s), 0)),
        out_shape=jax.ShapeDtypeStruct((t, w), BF16),
        scratch_shapes=[pltpu.VMEM((SUBLANES, w), F32), pltpu.VMEM((1, w), F32)],
        compiler_params=_cparams(("arbitrary", "arbitrary"), 48),
        name="conv_rglru",
    )(proj, proj, conv_w, vec(conv_b), w_r.astype(BF16), vec(b_r), w_i.astype(BF16), vec(b_i),
      vec(lam))


def _rotary(x, cos_ref, sin_lo_ref, sin_hi_ref, rows):
    up = pltpu.roll(x, HEAD_DIM - ROT_HALF, axis=1)
    down = pltpu.roll(x, ROT_HALF, axis=1)
    return x * cos_ref[rows, :] + up * sin_lo_ref[rows, :] + down * sin_hi_ref[rows, :]


_NT = (((1,), (1,)), ((), ()))


def _moba_kernel(q_ref, k_ref, v_ref, cos_ref, sin_lo_ref, sin_hi_ref, o_ref,
                 krot_ref, vt_ref, vtg_ref, kmean_ref, sel_ref, raw_ref):
    qi = pl.program_id(2)
    n_heads, n_blk = kmean_ref.shape[0], kmean_ref.shape[1]
    scale2 = HEAD_DIM ** -0.5 * 1.4426950408889634
    head_cols = [slice(hh * HEAD_DIM, (hh + 1) * HEAD_DIM) for hh in range(n_heads)]

    @pl.when(qi == 0)
    def _():
        for hh in range(n_heads):
            for n in range(n_blk):
                rows = pl.ds(n * MOBA_BLOCK, MOBA_BLOCK)
                kr = _rotary(k_ref[rows, head_cols[hh]].astype(F32), cos_ref, sin_lo_ref,
                             sin_hi_ref, rows)
                krot_ref[hh, rows, :] = kr.astype(krot_ref.dtype)
                kmean_ref[hh, n:n + 1, :] = jnp.mean(kr, axis=0, keepdims=True)
                vt = v_ref[rows, head_cols[hh]].astype(F32).T.astype(vt_ref.dtype)
                vt_ref[hh, n] = vt
                c = n % MOBA_SWEEP
                vtg_ref[hh, n // MOBA_SWEEP, :, c * MOBA_BLOCK:(c + 1) * MOBA_BLOCK] = vt

    own_rows = pl.ds(pl.multiple_of(qi * MOBA_BLOCK, MOBA_BLOCK), MOBA_BLOCK)

    qts, init = [], []
    for hh in range(n_heads):
        qt = _rotary(q_ref[:, head_cols[hh]].astype(F32), cos_ref, sin_lo_ref, sin_hi_ref,
                     own_rows).T.astype(BF16)
        qts.append(qt)

        gate = jnp.dot(kmean_ref[hh].astype(BF16), qt, preferred_element_type=F32)
        blk_i = lax.broadcasted_iota(jnp.int32, gate.shape, 0)
        blk = blk_i.astype(F32)
        past = blk_i < qi
        g = jnp.where(past, gate, -jnp.inf)
        sel = jnp.zeros_like(gate)
        for _ in range(MOBA_TOPK):
            best = jnp.max(g, axis=0, keepdims=True)
            idx = jnp.min(jnp.where(g == best, blk, float(n_blk)), axis=0, keepdims=True)
            pick = blk == idx
            sel = jnp.where(pick, 1.0, sel)
            g = jnp.where(pick, -jnp.inf, g)
        sel_ref[hh] = jnp.where(past, sel, 0.0)

        s = jnp.dot(krot_ref[hh, own_rows, :], qt, preferred_element_type=F32) * scale2
        key_id = lax.broadcasted_iota(jnp.int32, s.shape, 0)
        qry_id = lax.broadcasted_iota(jnp.int32, s.shape, 1)
        s = jnp.where(key_id <= qry_id, s, NEG_INF)
        m0 = jnp.max(s, axis=0, keepdims=True)
        p = jnp.exp2(s - m0)
        l0 = jnp.sum(p, axis=0, keepdims=True)
        acc0 = jnp.dot(vt_ref[hh, qi], p.astype(BF16), preferred_element_type=F32)
        init.append((m0, l0, acc0))

    span = MOBA_SWEEP * MOBA_BLOCK
    last_group = n_blk // MOBA_SWEEP - 1

    def raw_scores(hh, j):
        rows = pl.ds(pl.multiple_of(j * span, span), span)
        return jnp.dot(krot_ref[hh, rows, :], qts[hh], preferred_element_type=F32)

    for hh in range(n_heads):
        raw_ref[hh] = raw_scores(hh, 0)

    def past_group(j, carry):
        out = []
        for hh in range(n_heads):
            m, l, acc = carry[hh]
            s = jnp.concatenate([
                jnp.where(sel_ref[hh, pl.ds(j * MOBA_SWEEP + c, 1), :] > 0.0,
                          raw_ref[hh, c * MOBA_BLOCK:(c + 1) * MOBA_BLOCK, :] * scale2, NEG_INF)
                for c in range(MOBA_SWEEP)], axis=0)
            raw_ref[hh] = raw_scores(hh, jnp.minimum(j + 1, last_group))
            m_new = jnp.maximum(m, jnp.max(s, axis=0, keepdims=True))
            alpha = jnp.exp2(m - m_new)
            p = jnp.exp2(s - m_new)
            l = alpha * l + jnp.sum(p, axis=0, keepdims=True)
            acc = alpha * acc + jnp.dot(vtg_ref[hh, j], p.astype(BF16), preferred_element_type=F32)
            out.append((m_new, l, acc))
        return tuple(out)

    n_groups = lax.div(qi + (MOBA_SWEEP - 1), MOBA_SWEEP)
    final = lax.fori_loop(0, n_groups, past_group, tuple(init))
    for hh in range(n_heads):
        _, l, acc = final[hh]
        o_ref[:, head_cols[hh]] = (acc / l).T.astype(o_ref.dtype)


def _rope_tables(seq):
    pos = jnp.arange(seq, dtype=F32)
    inv_freq = ROPE_THETA ** (-jnp.arange(0, ROT_DIM, 2, dtype=F32) / ROT_DIM)
    ang = pos[:, None] * inv_freq[None, :]
    cos, sin = jnp.cos(ang), jnp.sin(ang)
    rest = HEAD_DIM - ROT_DIM
    cos_t = jnp.concatenate([cos, cos, jnp.ones((seq, rest), F32)], axis=1)
    sin_lo = jnp.concatenate([-sin, jnp.zeros((seq, HEAD_DIM - ROT_HALF), F32)], axis=1)
    sin_hi = jnp.concatenate([jnp.zeros((seq, ROT_HALF), F32), sin, jnp.zeros((seq, rest), F32)], axis=1)
    return cos_t, sin_lo, sin_hi


def _moba(proj, tables, batch, seq, heads, q_col, k_col, v_col, hg=MOBA_HEADS_PER_STEP):
    t = proj.shape[0]
    n_blk = seq // MOBA_BLOCK
    gw = hg * HEAD_DIM
    table_spec = _resident((seq, HEAD_DIM), lambda b, h, qi: (0, 0))
    return pl.pallas_call(
        _moba_kernel,
        grid=(batch, heads // hg, n_blk),
        in_specs=[
            pl.BlockSpec((MOBA_BLOCK, gw), lambda b, h, qi: (b * n_blk + qi, q_col + h)),
            pl.BlockSpec((seq, gw), lambda b, h, qi: (b, k_col + h)),
            pl.BlockSpec((seq, gw), lambda b, h, qi: (b, v_col + h)),
            table_spec, table_spec, table_spec,
        ],
        out_specs=pl.BlockSpec((MOBA_BLOCK, gw), lambda b, h, qi: (b * n_blk + qi, h)),
        out_shape=jax.ShapeDtypeStruct((t, heads * HEAD_DIM), BF16),
        scratch_shapes=[
            pltpu.VMEM((hg, seq, HEAD_DIM), BF16),
            pltpu.VMEM((hg, n_blk, HEAD_DIM, MOBA_BLOCK), BF16),
            pltpu.VMEM((hg, n_blk // MOBA_SWEEP, HEAD_DIM, MOBA_SWEEP * MOBA_BLOCK), BF16),
            pltpu.VMEM((hg, n_blk, HEAD_DIM), F32),
            pltpu.VMEM((hg, n_blk, MOBA_BLOCK), F32),
            pltpu.VMEM((hg, MOBA_SWEEP * MOBA_BLOCK, MOBA_BLOCK), F32),
        ],
        compiler_params=_cparams(("arbitrary", "arbitrary", "arbitrary"), 48),
        name="moba_attention",
    )(proj, proj, proj, *tables)


def _deepnorm_ln(x, y, ep, ln_g, ln_b, alpha):
    r = alpha * x + (1.0 + ep[0:1]) * y
    mu = jnp.mean(r, axis=-1, keepdims=True)
    rc = r - mu
    var = jnp.mean(rc * rc, axis=-1, keepdims=True)
    return rc * lax.rsqrt(var + LN_EPS) * ln_g + ln_b


def _next_h(x_new, ep):
    return x_new * (1.0 + ep[1:2]) + ep[2:3]


def _mix_kernel(alpha, ya_ref, at_ref, ga0_ref, ga1_ref, gb0_ref, gb1_ref, x_ref, ep_ref,
                wpa_ref, wpb_ref, wout_ref, lng_ref, lnb_ref, xo_ref, ho_ref):
    y_a = jnp.dot(ya_ref[...], wpa_ref[...], preferred_element_type=F32)
    y_b = jnp.dot(at_ref[...], wpb_ref[...], preferred_element_type=F32)
    gate_a = jnp.concatenate([ga0_ref[...], ga1_ref[...]], axis=1).astype(F32)
    gate_b = jnp.concatenate([gb0_ref[...], gb1_ref[...]], axis=1).astype(F32)
    merged = jax.nn.sigmoid(gate_a) * y_a + jax.nn.sigmoid(gate_b) * y_b
    y = jnp.dot(merged.astype(BF16), wout_ref[...], preferred_element_type=F32)
    ep = ep_ref[0]
    x_new = _deepnorm_ln(x_ref[...], y, ep, lng_ref[...], lnb_ref[...], alpha)
    xo_ref[...] = x_new
    ho_ref[...] = _next_h(x_new, ep).astype(ho_ref.dtype)


def _mix(ya, attn, proj, x2d, ep, w_pa, w_pb, w_out, ln_g, ln_b, alpha, seq, ga_col, gb_col, tm=256):
    t, d = x2d.shape
    wa = ya.shape[1]
    half = d // 2
    per_b = seq // tm
    const = lambda i: (0, 0)
    return pl.pallas_call(
        functools.partial(_mix_kernel, alpha),
        grid=(t // tm,),
        in_specs=[
            pl.BlockSpec((tm, wa), lambda i: (i, 0)),
            pl.BlockSpec((tm, wa), lambda i: (i, 0)),
            pl.BlockSpec((tm, half), lambda i: (i, ga_col)),
            pl.BlockSpec((tm, half), lambda i: (i, ga_col + 1)),
            pl.BlockSpec((tm, half), lambda i: (i, gb_col)),
            pl.BlockSpec((tm, half), lambda i: (i, gb_col + 1)),
            pl.BlockSpec((tm, d), lambda i: (i, 0)),
            pl.BlockSpec((1, 3, d), lambda i: (i // per_b, 0, 0)),
            _resident((wa, d), const),
            _resident((wa, d), const),
            _resident((d, d), const),
            _resident((1, d), const),
            _resident((1, d), const),
        ],
        out_specs=[pl.BlockSpec((tm, d), lambda i: (i, 0)), pl.BlockSpec((tm, d), lambda i: (i, 0))],
        out_shape=[jax.ShapeDtypeStruct((t, d), F32), jax.ShapeDtypeStruct((t, d), BF16)],
        compiler_params=_cparams(("arbitrary",), 56),
        name="mix_out_ln",
    )(ya, attn, proj, proj, proj, proj, x2d, ep, w_pa, w_pb, w_out, ln_g.reshape(1, d),
      ln_b.reshape(1, d))


def _swiglu_step(h, wg, wu, wd):
    g = jnp.dot(h, wg, preferred_element_type=F32)
    u = jnp.dot(h, wu, preferred_element_type=F32)
    return jnp.dot((g * jax.nn.sigmoid(g) * u).astype(BF16), wd, preferred_element_type=F32)


def _ffn_kernel(alpha, h_ref, wg_ref, wu_ref, wd_ref, x_ref, ep_ref, lng_ref, lnb_ref,
                xo_ref, ho_ref, acc_ref):
    f = pl.program_id(1)

    def partial_out():
        return _swiglu_step(h_ref[...], wg_ref[...], wu_ref[...], wd_ref[...])

    @pl.when(f == 0)
    def _():
        acc_ref[...] = partial_out()

    @pl.when(f > 0)
    def _():
        acc_ref[...] += partial_out()

    @pl.when(f == pl.num_programs(1) - 1)
    def _():
        ep = ep_ref[0]
        x_new = _deepnorm_ln(x_ref[...], acc_ref[...], ep, lng_ref[...], lnb_ref[...], alpha)
        xo_ref[...] = x_new
        ho_ref[...] = _next_h(x_new, ep).astype(ho_ref.dtype)


def _ffn(h, x2d, ep, w_gate, w_up, w_down, ln_g, ln_b, alpha, seq, tm=512, tf=512):
    t, d = x2d.shape
    ff = w_gate.shape[1]
    per_b = seq // tm
    return pl.pallas_call(
        functools.partial(_ffn_kernel, alpha),
        grid=(t // tm, ff // tf),
        in_specs=[
            pl.BlockSpec((tm, d), lambda i, f: (i, 0)),
            pl.BlockSpec((d, tf), lambda i, f: (0, f)),
            pl.BlockSpec((d, tf), lambda i, f: (0, f)),
            pl.BlockSpec((tf, d), lambda i, f: (f, 0)),
            pl.BlockSpec((tm, d), lambda i, f: (i, 0)),
            pl.BlockSpec((1, 3, d), lambda i, f: (i // per_b, 0, 0)),
            pl.BlockSpec((1, d), lambda i, f: (0, 0)),
            pl.BlockSpec((1, d), lambda i, f: (0, 0)),
        ],
        out_specs=[pl.BlockSpec((tm, d), lambda i, f: (i, 0)), pl.BlockSpec((tm, d), lambda i, f: (i, 0))],
        out_shape=[jax.ShapeDtypeStruct((t, d), F32), jax.ShapeDtypeStruct((t, d), BF16)],
        scratch_shapes=[pltpu.VMEM((tm, d), F32)],
        compiler_params=_cparams(("arbitrary", "arbitrary"), 56),
        name="ffn_dense_ln",
    )(h, w_gate, w_up, w_down, x2d, ep, ln_g.reshape(1, d), ln_b.reshape(1, d))


def _router_kernel(h_ref, w_ref, b_ref, o_ref, hp_ref):
    logits = jnp.dot(h_ref[...], w_ref[...], preferred_element_type=F32) + b_ref[...]
    lane_i = lax.broadcasted_iota(jnp.int32, logits.shape, 1)
    lane = lane_i.astype(F32)
    lg = jnp.where(lane_i < N_EXPERTS, logits, -jnp.inf)
    m1 = jnp.max(lg, axis=1, keepdims=True)
    i1 = jnp.min(jnp.where(lg == m1, lane, float(LANES)), axis=1, keepdims=True)
    lg2 = jnp.where(lane == i1, -jnp.inf, lg)
    m2 = jnp.max(lg2, axis=1, keepdims=True)
    i2 = jnp.min(jnp.where(lg2 == m2, lane, float(LANES)), axis=1, keepdims=True)
    e2 = jnp.exp(m2 - m1)
    denom = 1.0 + e2
    out = jnp.where(lane_i == 0, i1, 0.0)
    out = jnp.where(lane_i == 1, i2, out)
    out = jnp.where(lane_i == 2, 1.0 / denom, out)
    out = jnp.where(lane_i == 3, e2 / denom, out)
    o_ref[...] = out
    _store_token_tiles(hp_ref, _pack_bf16_pairs(h_ref[...]))


def _pack_bf16_pairs(h):
    half = h.shape[1] // 2
    bits = pltpu.bitcast(h.astype(F32), jnp.uint32)
    return bits[:, :half] | lax.shift_right_logical(bits[:, half:], jnp.uint32(16))


def _store_token_tiles(ref, rows):
    m = rows.shape[0]
    n = rows.shape[1] // LANES
    for c in range(n):
        ref[pl.ds(c, m, stride=n), :] = rows[:, c * LANES:(c + 1) * LANES]


def _load_token_tiles(ref, n):
    m = ref.shape[0] // n
    return jnp.concatenate([ref[pl.ds(c, m, stride=n), :] for c in range(n)], axis=1)


def _unpack_bf16_pairs(packed):
    hi = pltpu.bitcast(packed & jnp.uint32(0xFFFF0000), F32)
    lo = pltpu.bitcast(lax.shift_left(packed, jnp.uint32(16)), F32)
    return jnp.concatenate([hi, lo], axis=1).astype(BF16)


def _router(h, w_router_pad, b_router_pad, tm=1024):
    t, d = h.shape
    return pl.pallas_call(
        _router_kernel,
        grid=(t // tm,),
        in_specs=[
            pl.BlockSpec((tm, d), lambda i: (i, 0)),
            pl.BlockSpec((d, LANES), lambda i: (0, 0)),
            pl.BlockSpec((1, LANES), lambda i: (0, 0)),
        ],
        out_specs=[pl.BlockSpec((tm, LANES), lambda i: (i, 0)),
                   pl.BlockSpec((tm * (d // 2 // LANES), LANES), lambda i: (i, 0))],
        out_shape=[jax.ShapeDtypeStruct((t, LANES), F32),
                   jax.ShapeDtypeStruct((t * (d // 2 // LANES), LANES), jnp.uint32)],
        compiler_params=_cparams(("arbitrary",), 32),
        name="moe_router",
    )(h, w_router_pad, b_router_pad)


def _gmm_kernel(step_rows, te_ref, tv_ref, src_ref, dst_ref, hp_ref, wg_ref, wu_ref, wd_ref, y_ref,
                xbuf_ref, x_ref, acc_ref, ybuf_ref, gather_sem, scatter_sem):
    i = pl.program_id(0)
    f = pl.program_id(1)
    n_tiles = pl.num_programs(0)
    n_f = pl.num_programs(1)
    tm = acc_ref.shape[0]
    n_in = xbuf_ref.shape[1] // tm
    n_out = ybuf_ref.shape[0] // tm
    valid = tv_ref[i] > 0
    slot = lax.rem(i, 2)
    prev_valid = jnp.logical_and(i > 0, tv_ref[jnp.maximum(i - 1, 0)] > 0)
    nxt = jnp.minimum(i + 1, n_tiles - 1)
    next_valid = jnp.logical_and(i + 1 < n_tiles, tv_ref[nxt] > 0)

    lo = jnp.minimum(f * step_rows, tm)
    hi = jnp.minimum(lo + step_rows, tm)

    def tile_rows(row, n):
        return pl.ds(pl.multiple_of(row * n, n), n)

    def gather_rows(tile, to_slot, first, stop):
        base = tile * tm

        def row(r, carry):
            pltpu.make_async_copy(hp_ref.at[tile_rows(src_ref[base + r], n_in), :],
                                  xbuf_ref.at[to_slot, tile_rows(r, n_in), :],
                                  gather_sem.at[to_slot]).start()
            return carry

        lax.fori_loop(first, stop, row, 0)

    def gather_wait(from_slot):
        pltpu.make_async_copy(hp_ref.at[pl.ds(0, tm * n_in), :], xbuf_ref.at[from_slot],
                              gather_sem.at[from_slot]).wait()

    def scatter_rows(tile, first, stop):
        base = tile * tm

        def row(r, carry):
            pltpu.make_async_copy(ybuf_ref.at[tile_rows(r, n_out), :],
                                  y_ref.at[tile_rows(dst_ref[base + r], n_out), :],
                                  scatter_sem.at[0]).start()
            return carry

        lax.fori_loop(first, stop, row, 0)

    def scatter_wait():
        pltpu.make_async_copy(ybuf_ref, y_ref.at[pl.ds(0, tm * n_out), :],
                              scatter_sem.at[0]).wait()

    @pl.when(jnp.logical_and(f == 0, i == 0))
    def _():
        gather_rows(0, 0, 0, tm)

    @pl.when(valid)
    def _():
        @pl.when(f == 0)
        def _():
            gather_wait(slot)
            x_ref[...] = _unpack_bf16_pairs(_load_token_tiles(xbuf_ref.at[slot], n_in))

        @pl.when(next_valid)
        def _():
            gather_rows(nxt, 1 - slot, lo, hi)

        @pl.when(prev_valid)
        def _():
            scatter_rows(i - 1, lo, hi)

        def partial_out():
            return _swiglu_step(x_ref[...], wg_ref[...].astype(BF16), wu_ref[...].astype(BF16),
                                wd_ref[...].astype(BF16))

        @pl.when(f == 0)
        def _():
            acc_ref[...] = partial_out()

        @pl.when(f > 0)
        def _():
            acc_ref[...] += partial_out()

        @pl.when(f == n_f - 1)
        def _():
            @pl.when(prev_valid)
            def _():
                scatter_wait()

            _store_token_tiles(ybuf_ref, acc_ref[...])

            @pl.when(jnp.logical_not(next_valid))
            def _():
                scatter_rows(i, 0, tm)
                scatter_wait()


def _gmm(tile_expert, tile_valid, src_token, dst_row, hp, w_gate, w_up, w_down, layer, n_out_rows,
         tm, tf=256):
    r = src_token.shape[0]
    d = w_gate.shape[2]
    ff = w_gate.shape[3]
    nf = ff // tf

    def f_idx(i, f, tv):
        return jnp.where(tv[i] > 0, f, nf - 1)

    grid_spec = pltpu.PrefetchScalarGridSpec(
        num_scalar_prefetch=4,
        grid=(r // tm, nf),
        in_specs=[
            pl.BlockSpec(memory_space=pl.ANY),
            pl.BlockSpec((None, None, d, tf),
                         lambda i, f, te, tv, src, dst: (layer, te[i], 0, f_idx(i, f, tv))),
            pl.BlockSpec((None, None, d, tf),
                         lambda i, f, te, tv, src, dst: (layer, te[i], 0, f_idx(i, f, tv))),
            pl.BlockSpec((None, None, tf, d),
                         lambda i, f, te, tv, src, dst: (layer, te[i], f_idx(i, f, tv), 0)),
        ],
        out_specs=pl.BlockSpec(memory_space=pl.ANY),
        scratch_shapes=[
            pltpu.VMEM((2, tm * (d // 2 // LANES), LANES), jnp.uint32),
            pltpu.VMEM((tm, d), BF16),
            pltpu.VMEM((tm, d), F32),
            pltpu.VMEM((tm * (d // LANES), LANES), F32),
            pltpu.SemaphoreType.DMA((2,)),
            pltpu.SemaphoreType.DMA((1,)),
        ],
    )
    step_rows = pl.cdiv(pl.cdiv(tm, nf), SUBLANES) * SUBLANES
    return pl.pallas_call(
        functools.partial(_gmm_kernel, step_rows),
        grid_spec=grid_spec,
        out_shape=jax.ShapeDtypeStruct((n_out_rows * (d // LANES), LANES), F32),
        compiler_params=pltpu.CompilerParams(
            dimension_semantics=("arbitrary", "arbitrary"), vmem_limit_bytes=56 * MIB,
            disable_bounds_checks=True),
        name="moe_grouped_swiglu",
    )(tile_expert, tile_valid, src_token, dst_row, hp, w_gate, w_up, w_down)


def _combine_ln_kernel(alpha, route_ref, y1_ref, y2_ref, x_ref, ep_ref, lng_ref, lnb_ref,
                       xo_ref, ho_ref):
    ep = ep_ref[0]
    route = route_ref[...]
    n = x_ref.shape[1] // LANES
    y = (route[:, 2:3] * _load_token_tiles(y1_ref, n)
         + route[:, 3:4] * _load_token_tiles(y2_ref, n))
    x_new = _deepnorm_ln(x_ref[...], y, ep, lng_ref[...], lnb_ref[...], alpha)
    xo_ref[...] = x_new
    ho_ref[...] = _next_h(x_new, ep).astype(ho_ref.dtype)


def _combine_ln(route, y_slots, x2d, ep, ln_g, ln_b, alpha, seq, tm=256):
    t, d = x2d.shape
    per_b = seq // tm
    n_tiles = t // tm
    return pl.pallas_call(
        functools.partial(_combine_ln_kernel, alpha),
        grid=(n_tiles,),
        in_specs=[
            pl.BlockSpec((tm, LANES), lambda i: (i, 0)),
            pl.BlockSpec((tm * (d // LANES), LANES), lambda i: (i, 0)),
            pl.BlockSpec((tm * (d // LANES), LANES), lambda i: (i + n_tiles, 0)),
            pl.BlockSpec((tm, d), lambda i: (i, 0)),
            pl.BlockSpec((1, 3, d), lambda i: (i // per_b, 0, 0)),
            pl.BlockSpec((1, d), lambda i: (0, 0)),
            pl.BlockSpec((1, d), lambda i: (0, 0)),
        ],
        out_specs=[pl.BlockSpec((tm, d), lambda i: (i, 0)), pl.BlockSpec((tm, d), lambda i: (i, 0))],
        out_shape=[jax.ShapeDtypeStruct((t, d), F32), jax.ShapeDtypeStruct((t, d), BF16)],
        compiler_params=_cparams(("arbitrary",), 48),
        name="moe_combine_ln",
    )(route, y_slots, y_slots, x2d, ep, ln_g.reshape(1, d), ln_b.reshape(1, d))


def _dispatch_plan(route, tm):
    t = route.shape[0]
    n_pairs = 2 * t
    n_rows = n_pairs + N_EXPERTS * tm
    expert = route[:, :2].astype(jnp.int32).reshape(n_pairs)
    onehot = (expert[:, None] == jnp.arange(N_EXPERTS, dtype=jnp.int32)[None, :]).astype(jnp.int32)
    rank = jnp.take_along_axis(jnp.cumsum(onehot, axis=0) - onehot, expert[:, None], axis=1)[:, 0]
    counts = jnp.sum(onehot, axis=0)
    padded = ((counts + tm - 1) // tm) * tm
    ends = jnp.cumsum(padded)
    starts = ends - padded
    pos = starts[expert] + rank
    pair = jnp.full((n_rows,), -1, jnp.int32).at[pos].set(jnp.arange(n_pairs, dtype=jnp.int32))
    token = jnp.maximum(pair, 0) // 2
    is_pad = (pair < 0).astype(jnp.int32)
    src_token = jnp.where(pair >= 0, token, 0)
    dst_row = jnp.where(pair >= 0, (pair % 2) * t + token, n_pairs + jnp.cumsum(is_pad) - 1)
    tile_start = jnp.arange(n_rows // tm, dtype=jnp.int32) * tm
    tile_valid = (tile_start < ends[-1]).astype(jnp.int32)
    n_ended = jnp.sum((tile_start[:, None] >= ends[None, :]).astype(jnp.int32), axis=1)
    tile_expert = jnp.minimum(n_ended, N_EXPERTS - 1)
    last_valid_expert = jnp.max(jnp.where(tile_valid > 0, tile_expert, 0))
    tile_expert = jnp.where(tile_valid > 0, tile_expert, last_valid_expert)
    return src_token, dst_row, tile_expert, tile_valid


def kernel(x, c, w_in, conv_w, conv_b, w_rgate, b_rgate, w_igate, b_igate, lru_lambda, w_proj_a, w_proj_b, w_out, w_ada, b_ada, ln_mix_g, ln_mix_b, ln_ffn_g, ln_ffn_b, ffn_w_gate, ffn_w_up, ffn_w_down, moe_w_router, moe_b_router, moe_w_gate, moe_w_up, moe_w_down):
    batch, seq, d = x.shape
    depth = w_in.shape[0]
    t = batch * seq
    lru_w = conv_w.shape[2]
    attn_w = w_proj_b.shape[1]
    heads = attn_w // HEAD_DIM
    alpha = (2.0 * depth) ** 0.25
    moe_tm = 1024

    c_pad = jnp.zeros((SUBLANES, d), F32).at[:batch].set(c)
    ada = _ada_all_layers(c_pad, w_ada, b_ada)[:, :batch].reshape(depth, batch, 6, d)
    zeros_bd = jnp.zeros((batch, d), F32)

    def epilogue_params(gate, scale, shift):
        return jnp.stack([gate, scale, shift], axis=1)

    tables = _rope_tables(seq)
    x2d = x.reshape(t, d)
    h = _modulate(x2d, epilogue_params(zeros_bd, ada[0, :, 1], ada[0, :, 0]), seq)

    head_group_w = MOBA_HEADS_PER_STEP * HEAD_DIM
    q_col = 2 * lru_w // head_group_w
    k_col = q_col + attn_w // head_group_w
    v_col = k_col + attn_w // head_group_w
    ga_col = (2 * lru_w + 3 * attn_w) // (d // 2)
    gb_col = ga_col + 2

    for l in range(depth):
        sh1, sc1, g1, sh2, sc2, g2 = (ada[l, :, j] for j in range(6))
        proj = _matmul(h, w_in, l)
        ya = _lru(proj, conv_w[l], conv_b[l], w_rgate[l], b_rgate[l], w_igate[l], b_igate[l],
                  lru_lambda[l], batch, seq)
        attn = _moba(proj, tables, batch, seq, heads, q_col, k_col, v_col)
        x2d, h = _mix(ya, attn, proj, x2d, epilogue_params(g1, sc2, sh2),
                      w_proj_a[l].astype(BF16), w_proj_b[l].astype(BF16), w_out[l].astype(BF16),
                      ln_mix_g[l], ln_mix_b[l], alpha, seq, ga_col, gb_col)
        if l + 1 < depth:
            ep = epilogue_params(g2, ada[l + 1, :, 1], ada[l + 1, :, 0])
        else:
            ep = epilogue_params(g2, zeros_bd, zeros_bd)
        j = l // 2
        if l % 2 == 0:
            x2d, h = _ffn(h, x2d, ep, ffn_w_gate[j].astype(BF16), ffn_w_up[j].astype(BF16),
                          ffn_w_down[j].astype(BF16), ln_ffn_g[l], ln_ffn_b[l], alpha, seq)
        else:
            w_r_pad = jnp.zeros((d, LANES), BF16).at[:, :N_EXPERTS].set(moe_w_router[j].astype(BF16))
            b_r_pad = jnp.zeros((1, LANES), F32).at[0, :N_EXPERTS].set(moe_b_router[j])
            route, h_packed = _router(h, w_r_pad, b_r_pad)
            src_token, dst_row, tile_expert, tile_valid = _dispatch_plan(route, moe_tm)
            y_slots = _gmm(tile_expert, tile_valid, src_token, dst_row, h_packed, moe_w_gate,
                           moe_w_up, moe_w_down, j, src_token.shape[0], moe_tm)
            x2d, h = _combine_ln(route, y_slots, x2d, ep, ln_ffn_g[l], ln_ffn_b[l], alpha, seq)
    return x2d.reshape(batch, seq, d)
```

```python
import functools

import jax
import jax.numpy as jnp
from jax import lax
from jax.experimental import pallas as pl
from jax.experimental.pallas import tpu as pltpu

F32 = jnp.float32
BF16 = jnp.bfloat16

LRU_BLOCKS = 8
LRU_BLOCK_W = 128
CONV_WIDTH = 4
LRU_C = 8.0
HEAD_DIM = 128
MOBA_BLOCK = 256
MOBA_TOPK = 3
MOBA_HEADS_PER_STEP = 4
MOBA_SWEEP = 4
ROPE_THETA = 500000.0
ROT_DIM = HEAD_DIM // 4
ROT_HALF = ROT_DIM // 2
NEG_INF = -1e30
N_EXPERTS = 8
LN_EPS = 1e-5
LANES = 128
SUBLANES = 8
MIB = 1024 * 1024


def _cparams(semantics, vmem_mib):
    return pltpu.CompilerParams(dimension_semantics=semantics, vmem_limit_bytes=vmem_mib * MIB)


def _resident(block_shape, index_map):
    return pl.BlockSpec(block_shape, index_map, pipeline_mode=pl.Buffered(1))


def _ada_kernel(c_ref, w_ref, b_ref, o_ref):
    c = c_ref[...]
    c_act = c * jax.nn.sigmoid(c)
    o_ref[0] = jnp.dot(c_act, w_ref[0], preferred_element_type=F32) + b_ref[0]


def _ada_all_layers(c_pad, w_ada, b_ada, tn=1024):
    depth, d, n = w_ada.shape
    rows = c_pad.shape[0]
    return pl.pallas_call(
        _ada_kernel,
        grid=(depth, n // tn),
        in_specs=[
            pl.BlockSpec((rows, d), lambda l, j: (0, 0)),
            pl.BlockSpec((1, d, tn), lambda l, j: (l, 0, j)),
            pl.BlockSpec((1, 1, tn), lambda l, j: (l, 0, j)),
        ],
        out_specs=pl.BlockSpec((1, rows, tn), lambda l, j: (l, 0, j)),
        out_shape=jax.ShapeDtypeStruct((depth, rows, n), F32),
        compiler_params=_cparams(("arbitrary", "arbitrary"), 40),
        name="ada",
    )(c_pad, w_ada, b_ada.reshape(depth, 1, n))


def _modulate_kernel(x_ref, ep_ref, h_ref):
    ep = ep_ref[0]
    h_ref[...] = (x_ref[...] * (1.0 + ep[1:2]) + ep[2:3]).astype(h_ref.dtype)


def _modulate(x2d, ep, seq, tm=512):
    t, d = x2d.shape
    per_b = seq // tm
    return pl.pallas_call(
        _modulate_kernel,
        grid=(t // tm,),
        in_specs=[
            pl.BlockSpec((tm, d), lambda i: (i, 0)),
            pl.BlockSpec((1, 3, d), lambda i: (i // per_b, 0, 0)),
        ],
        out_specs=pl.BlockSpec((tm, d), lambda i: (i, 0)),
        out_shape=jax.ShapeDtypeStruct((t, d), BF16),
        compiler_params=_cparams(("arbitrary",), 32),
        name="modulate",
    )(x2d, ep)


def _matmul_kernel(a_ref, w_ref, o_ref, wb_ref):
    @pl.when(pl.program_id(1) == 0)
    def _():
        wb_ref[...] = w_ref[...].astype(wb_ref.dtype)

    o_ref[...] = jnp.dot(a_ref[...], wb_ref[...], preferred_element_type=F32).astype(o_ref.dtype)


def _matmul(a, w, layer, tm=1024, tn=1024):
    m, k = a.shape
    n = w.shape[2]
    return pl.pallas_call(
        _matmul_kernel,
        grid=(n // tn, m // tm),
        in_specs=[
            pl.BlockSpec((tm, k), lambda j, i: (i, 0)),
            pl.BlockSpec((None, k, tn), lambda j, i: (layer, 0, j)),
        ],
        out_specs=pl.BlockSpec((tm, tn), lambda j, i: (i, j)),
        out_shape=jax.ShapeDtypeStruct((m, n), BF16),
        scratch_shapes=[pltpu.VMEM((k, tn), BF16)],
        compiler_params=_cparams(("arbitrary", "arbitrary"), 48),
        name="in_proj",
    )(a, w)


def _gelu_tanh(x):
    return 0.5 * x * (1.0 + jnp.tanh(0.7978845608028654 * (x + 0.044715 * (x * x * x))))


def _lru_kernel(x_ref, g_ref, cw_ref, cb_ref, wr_ref, br_ref, wi_ref, bi_ref, lam_ref,
                o_ref, tail_ref, carry_ref):
    @pl.when(pl.program_id(1) == 0)
    def _():
        tail_ref[...] = jnp.zeros_like(tail_ref)
        carry_ref[...] = jnp.zeros_like(carry_ref)

    ts, w = x_ref.shape
    groups = ts // SUBLANES
    x3 = x_ref[...].astype(F32).reshape(groups, SUBLANES, w)
    xe = jnp.concatenate([tail_ref[...][None], x3], axis=0)
    sub = lax.broadcasted_iota(jnp.int32, (groups, SUBLANES, w), 1)

    y = cb_ref[...][None] + cw_ref[CONV_WIDTH - 1:CONV_WIDTH, :][None] * x3
    for k in range(1, CONV_WIDTH):
        rolled = pltpu.roll(xe, k, axis=1)
        shifted = jnp.where(sub < k, rolled[:-1], rolled[1:])
        y = y + cw_ref[CONV_WIDTH - 1 - k:CONV_WIDTH - k, :][None] * shifted
    tail_ref[...] = x3[groups - 1]

    xa = y.reshape(ts, w)
    xab = xa.astype(BF16)

    def block_diag(w_ref, b_ref):
        parts = [
            jnp.dot(xab[:, n * LRU_BLOCK_W:(n + 1) * LRU_BLOCK_W], w_ref[n],
                    preferred_element_type=F32)
            for n in range(LRU_BLOCKS)
        ]
        return jnp.concatenate(parts, axis=1) + b_ref[...]

    r = jax.nn.sigmoid(block_diag(wr_ref, br_ref))
    i = jax.nn.sigmoid(block_diag(wi_ref, bi_ref))
    neg_lam = -lam_ref[...]
    softplus = jnp.maximum(neg_lam, 0.0) + jnp.log1p(jnp.exp(-jnp.abs(neg_lam)))
    a = jnp.exp((-LRU_C) * r * softplus)
    u = jnp.sqrt(1.0 - a * a) * (i * xa)

    a3 = a.reshape(groups, SUBLANES, w)
    u3 = u.reshape(groups, SUBLANES, w)
    d = 1
    while d < SUBLANES:
        a_prev = pltpu.roll(a3, d, axis=1)
        u_prev = pltpu.roll(u3, d, axis=1)
        take = sub >= d
        u3 = jnp.where(take, a3 * u_prev + u3, u3)
        a3 = jnp.where(take, a3 * a_prev, a3)
        d *= 2
    carry = carry_ref[...]
    hs = []
    for gi in range(groups):
        hg = u3[gi] + a3[gi] * carry
        hs.append(hg)
        carry = hg[SUBLANES - 1:SUBLANES, :]
    carry_ref[...] = carry
    h = jnp.stack(hs, axis=0).reshape(ts, w)

    o_ref[...] = (h * _gelu_tanh(g_ref[...].astype(F32))).astype(o_ref.dtype)


def _lru(proj, conv_w, conv_b, w_r, b_r, w_i, b_i, lam, batch, seq, ts=256):
    t = proj.shape[0]
    w = conv_w.shape[1]
    per_b = seq // ts
    row = lambda b, s: b * per_b + s
    vec = lambda v: v.reshape(1, w)
    const2 = lambda b, s: (0, 0)
    const3 = lambda b, s: (0, 0, 0)
    return pl.pallas_call(
        _lru_kernel,
        grid=(batch, per_b),
        in_specs=[
            pl.BlockSpec((ts, w), lambda b, s: (row(b, s), 0)),
            pl.BlockSpec((ts, w), lambda b, s: (row(b, s), 1)),
            pl.BlockSpec((CONV_WIDTH, w), const2),
            pl.BlockSpec((1, w), const2),
            pl.BlockSpec((LRU_BLOCKS, LRU_BLOCK_W, LRU_BLOCK_W), const3),
            pl.BlockSpec((1, w), const2),
            pl.BlockSpec((LRU_BLOCKS, LRU_BLOCK_W, LRU_BLOCK_W), const3),
            pl.BlockSpec((1, w), const2),
            pl.BlockSpec((1, w), const2),
        ],
        out_specs=pl.BlockSpec((ts, w), lambda b, s: (row(b, s), 0)),
        out_shape=jax.ShapeDtypeStruct((t, w), BF16),
        scratch_shapes=[pltpu.VMEM((SUBLANES, w), F32), pltpu.VMEM((1, w), F32)],
        compiler_params=_cparams(("arbitrary", "arbitrary"), 48),
        name="conv_rglru",
    )(proj, proj, conv_w, vec(conv_b), w_r.astype(BF16), vec(b_r), w_i.astype(BF16), vec(b_i),
      vec(lam))


def _rotary(x, cos_ref, sin_lo_ref, sin_hi_ref, rows):
    up = pltpu.roll(x, HEAD_DIM - ROT_HALF, axis=1)
    down = pltpu.roll(x, ROT_HALF, axis=1)
    return x * cos_ref[rows, :] + up * sin_lo_ref[rows, :] + down * sin_hi_ref[rows, :]


_NT = (((1,), (1,)), ((), ()))


def _moba_kernel(q_ref, k_ref, v_ref, cos_ref, sin_lo_ref, sin_hi_ref, o_ref,
                 krot_ref, vt_ref, vtg_ref, kmean_ref, sel_ref, raw_ref):
    qi = pl.program_id(2)
    n_heads, n_blk = kmean_ref.shape[0], kmean_ref.shape[1]
    scale2 = HEAD_DIM ** -0.5 * 1.4426950408889634
    head_cols = [slice(hh * HEAD_DIM, (hh + 1) * HEAD_DIM) for hh in range(n_heads)]

    @pl.when(qi == 0)
    def _():
        for hh in range(n_heads):
            for n in range(n_blk):
                rows = pl.ds(n * MOBA_BLOCK, MOBA_BLOCK)
                kr = _rotary(k_ref[rows, head_cols[hh]].astype(F32), cos_ref, sin_lo_ref,
                             sin_hi_ref, rows)
                krot_ref[hh, rows, :] = kr.astype(krot_ref.dtype)
                kmean_ref[hh, n:n + 1, :] = jnp.mean(kr, axis=0, keepdims=True)
                vt = v_ref[rows, head_cols[hh]].astype(F32).T.astype(vt_ref.dtype)
                vt_ref[hh, n] = vt
                c = n % MOBA_SWEEP
                vtg_ref[hh, n // MOBA_SWEEP, :, c * MOBA_BLOCK:(c + 1) * MOBA_BLOCK] = vt

    own_rows = pl.ds(pl.multiple_of(qi * MOBA_BLOCK, MOBA_BLOCK), MOBA_BLOCK)

    qts, init = [], []
    for hh in range(n_heads):
        qt = _rotary(q_ref[:, head_cols[hh]].astype(F32), cos_ref, sin_lo_ref, sin_hi_ref,
                     own_rows).T.astype(BF16)
        qts.append(qt)

        gate = jnp.dot(kmean_ref[hh].astype(BF16), qt, preferred_element_type=F32)
        blk_i = lax.broadcasted_iota(jnp.int32, gate.shape, 0)
        blk = blk_i.astype(F32)
        past = blk_i < qi
        g = jnp.where(past, gate, -jnp.inf)
        sel = jnp.zeros_like(gate)
        for _ in range(MOBA_TOPK):
            best = jnp.max(g, axis=0, keepdims=True)
            idx = jnp.min(jnp.where(g == best, blk, float(n_blk)), axis=0, keepdims=True)
            pick = blk == idx
            sel = jnp.where(pick, 1.0, sel)
            g = jnp.where(pick, -jnp.inf, g)
        sel_ref[hh] = jnp.where(past, sel, 0.0)

        s = jnp.dot(krot_ref[hh, own_rows, :], qt, preferred_element_type=F32) * scale2
        key_id = lax.broadcasted_iota(jnp.int32, s.shape, 0)
        qry_id = lax.broadcasted_iota(jnp.int32, s.shape, 1)
        s = jnp.where(key_id <= qry_id, s, NEG_INF)
        m0 = jnp.max(s, axis=0, keepdims=True)
        p = jnp.exp2(s - m0)
        l0 = jnp.sum(p, axis=0, keepdims=True)
        acc0 = jnp.dot(vt_ref[hh, qi], p.astype(BF16), preferred_element_type=F32)
        init.append((m0, l0, acc0))

    span = MOBA_SWEEP * MOBA_BLOCK
    last_group = n_blk // MOBA_SWEEP - 1

    def raw_scores(hh, j):
        rows = pl.ds(pl.multiple_of(j * span, span), span)
        return jnp.dot(krot_ref[hh, rows, :], qts[hh], preferred_element_type=F32)

    for hh in range(n_heads):
        raw_ref[hh] = raw_scores(hh, 0)

    def past_group(j, carry):
        out = []
        for hh in range(n_heads):
            m, l, acc = carry[hh]
            s = jnp.concatenate([
                jnp.where(sel_ref[hh, pl.ds(j * MOBA_SWEEP + c, 1), :] > 0.0,
                          raw_ref[hh, c * MOBA_BLOCK:(c + 1) * MOBA_BLOCK, :] * scale2, NEG_INF)
                for c in range(MOBA_SWEEP)], axis=0)
            raw_ref[hh] = raw_scores(hh, jnp.minimum(j + 1, last_group))
            m_new = jnp.maximum(m, jnp.max(s, axis=0, keepdims=True))
            alpha = jnp.exp2(m - m_new)
            p = jnp.exp2(s - m_new)
            l = alpha * l + jnp.sum(p, axis=0, keepdims=True)
            acc = alpha * acc + jnp.dot(vtg_ref[hh, j], p.astype(BF16), preferred_element_type=F32)
            out.append((m_new, l, acc))
        return tuple(out)

    n_groups = lax.div(qi + (MOBA_SWEEP - 1), MOBA_SWEEP)
    final = lax.fori_loop(0, n_groups, past_group, tuple(init))
    for hh in range(n_heads):
        _, l, acc = final[hh]
        o_ref[:, head_cols[hh]] = (acc / l).T.astype(o_ref.dtype)


def _rope_tables(seq):
    pos = jnp.arange(seq, dtype=F32)
    inv_freq = ROPE_THETA ** (-jnp.arange(0, ROT_DIM, 2, dtype=F32) / ROT_DIM)
    ang = pos[:, None] * inv_freq[None, :]
    cos, sin = jnp.cos(ang), jnp.sin(ang)
    rest = HEAD_DIM - ROT_DIM
    cos_t = jnp.concatenate([cos, cos, jnp.ones((seq, rest), F32)], axis=1)
    sin_lo = jnp.concatenate([-sin, jnp.zeros((seq, HEAD_DIM - ROT_HALF), F32)], axis=1)
    sin_hi = jnp.concatenate([jnp.zeros((seq, ROT_HALF), F32), sin, jnp.zeros((seq, rest), F32)], axis=1)
    return cos_t, sin_lo, sin_hi


def _moba(proj, tables, batch, seq, heads, q_col, k_col, v_col, hg=MOBA_HEADS_PER_STEP):
    t = proj.shape[0]
    n_blk = seq // MOBA_BLOCK
    gw = hg * HEAD_DIM
    table_spec = _resident((seq, HEAD_DIM), lambda b, h, qi: (0, 0))
    return pl.pallas_call(
        _moba_kernel,
        grid=(batch, heads // hg, n_blk),
        in_specs=[
            pl.BlockSpec((MOBA_BLOCK, gw), lambda b, h, qi: (b * n_blk + qi, q_col + h)),
            pl.BlockSpec((seq, gw), lambda b, h, qi: (b, k_col + h)),
            pl.BlockSpec((seq, gw), lambda b, h, qi: (b, v_col + h)),
            table_spec, table_spec, table_spec,
        ],
        out_specs=pl.BlockSpec((MOBA_BLOCK, gw), lambda b, h, qi: (b * n_blk + qi, h)),
        out_shape=jax.ShapeDtypeStruct((t, heads * HEAD_DIM), BF16),
        scratch_shapes=[
            pltpu.VMEM((hg, seq, HEAD_DIM), BF16),
            pltpu.VMEM((hg, n_blk, HEAD_DIM, MOBA_BLOCK), BF16),
            pltpu.VMEM((hg, n_blk // MOBA_SWEEP, HEAD_DIM, MOBA_SWEEP * MOBA_BLOCK), BF16),
            pltpu.VMEM((hg, n_blk, HEAD_DIM), F32),
            pltpu.VMEM((hg, n_blk, MOBA_BLOCK), F32),
            pltpu.VMEM((hg, MOBA_SWEEP * MOBA_BLOCK, MOBA_BLOCK), F32),
        ],
        compiler_params=_cparams(("arbitrary", "arbitrary", "arbitrary"), 48),
        name="moba_attention",
    )(proj, proj, proj, *tables)


def _deepnorm_ln(x, y, ep, ln_g, ln_b, alpha):
    r = alpha * x + (1.0 + ep[0:1]) * y
    mu = jnp.mean(r, axis=-1, keepdims=True)
    rc = r - mu
    var = jnp.mean(rc * rc, axis=-1, keepdims=True)
    return rc * lax.rsqrt(var + LN_EPS) * ln_g + ln_b


def _next_h(x_new, ep):
    return x_new * (1.0 + ep[1:2]) + ep[2:3]


def _mix_kernel(alpha, ya_ref, at_ref, ga0_ref, ga1_ref, gb0_ref, gb1_ref, x_ref, ep_ref,
                wpa_ref, wpb_ref, wout_ref, lng_ref, lnb_ref, xo_ref, ho_ref):
    y_a = jnp.dot(ya_ref[...], wpa_ref[...], preferred_element_type=F32)
    y_b = jnp.dot(at_ref[...], wpb_ref[...], preferred_element_type=F32)
    gate_a = jnp.concatenate([ga0_ref[...], ga1_ref[...]], axis=1).astype(F32)
    gate_b = jnp.concatenate([gb0_ref[...], gb1_ref[...]], axis=1).astype(F32)
    merged = jax.nn.sigmoid(gate_a) * y_a + jax.nn.sigmoid(gate_b) * y_b
    y = jnp.dot(merged.astype(BF16), wout_ref[...], preferred_element_type=F32)
    ep = ep_ref[0]
    x_new = _deepnorm_ln(x_ref[...], y, ep, lng_ref[...], lnb_ref[...], alpha)
    xo_ref[...] = x_new
    ho_ref[...] = _next_h(x_new, ep).astype(ho_ref.dtype)


def _mix(ya, attn, proj, x2d, ep, w_pa, w_pb, w_out, ln_g, ln_b, alpha, seq, ga_col, gb_col, tm=256):
    t, d = x2d.shape
    wa = ya.shape[1]
    half = d // 2
    per_b = seq // tm
    const = lambda i: (0, 0)
    return pl.pallas_call(
        functools.partial(_mix_kernel, alpha),
        grid=(t // tm,),
        in_specs=[
            pl.BlockSpec((tm, wa), lambda i: (i, 0)),
            pl.BlockSpec((tm, wa), lambda i: (i, 0)),
            pl.BlockSpec((tm, half), lambda i: (i, ga_col)),
            pl.BlockSpec((tm, half), lambda i: (i, ga_col + 1)),
            pl.BlockSpec((tm, half), lambda i: (i, gb_col)),
            pl.BlockSpec((tm, half), lambda i: (i, gb_col + 1)),
            pl.BlockSpec((tm, d), lambda i: (i, 0)),
            pl.BlockSpec((1, 3, d), lambda i: (i // per_b, 0, 0)),
            _resident((wa, d), const),
            _resident((wa, d), const),
            _resident((d, d), const),
            _resident((1, d), const),
            _resident((1, d), const),
        ],
        out_specs=[pl.BlockSpec((tm, d), lambda i: (i, 0)), pl.BlockSpec((tm, d), lambda i: (i, 0))],
        out_shape=[jax.ShapeDtypeStruct((t, d), F32), jax.ShapeDtypeStruct((t, d), BF16)],
        compiler_params=_cparams(("arbitrary",), 56),
        name="mix_out_ln",
    )(ya, attn, proj, proj, proj, proj, x2d, ep, w_pa, w_pb, w_out, ln_g.reshape(1, d),
      ln_b.reshape(1, d))


def _swiglu_step(h, wg, wu, wd):
    g = jnp.dot(h, wg, preferred_element_type=F32)
    u = jnp.dot(h, wu, preferred_element_type=F32)
    return jnp.dot((g * jax.nn.sigmoid(g) * u).astype(BF16), wd, preferred_element_type=F32)


def _ffn_kernel(alpha, h_ref, wg_ref, wu_ref, wd_ref, x_ref, ep_ref, lng_ref, lnb_ref,
                xo_ref, ho_ref, acc_ref):
    f = pl.program_id(1)

    def partial_out():
        return _swiglu_step(h_ref[...], wg_ref[...], wu_ref[...], wd_ref[...])

    @pl.when(f == 0)
    def _():
        acc_ref[...] = partial_out()

    @pl.when(f > 0)
    def _():
        acc_ref[...] += partial_out()

    @pl.when(f == pl.num_programs(1) - 1)
    def _():
        ep = ep_ref[0]
        x_new = _deepnorm_ln(x_ref[...], acc_ref[...], ep, lng_ref[...], lnb_ref[...], alpha)
        xo_ref[...] = x_new
        ho_ref[...] = _next_h(x_new, ep).astype(ho_ref.dtype)


def _ffn(h, x2d, ep, w_gate, w_up, w_down, ln_g, ln_b, alpha, seq, tm=512, tf=512):
    t, d = x2d.shape
    ff = w_gate.shape[1]
    per_b = seq // tm
    return pl.pallas_call(
        functools.partial(_ffn_kernel, alpha),
        grid=(t // tm, ff // tf),
        in_specs=[
            pl.BlockSpec((tm, d), lambda i, f: (i, 0)),
            pl.BlockSpec((d, tf), lambda i, f: (0, f)),
            pl.BlockSpec((d, tf), lambda i, f: (0, f)),
            pl.BlockSpec((tf, d), lambda i, f: (f, 0)),
            pl.BlockSpec((tm, d), lambda i, f: (i, 0)),
            pl.BlockSpec((1, 3, d), lambda i, f: (i // per_b, 0, 0)),
            pl.BlockSpec((1, d), lambda i, f: (0, 0)),
            pl.BlockSpec((1, d), lambda i, f: (0, 0)),
        ],
        out_specs=[pl.BlockSpec((tm, d), lambda i, f: (i, 0)), pl.BlockSpec((tm, d), lambda i, f: (i, 0))],
        out_shape=[jax.ShapeDtypeStruct((t, d), F32), jax.ShapeDtypeStruct((t, d), BF16)],
        scratch_shapes=[pltpu.VMEM((tm, d), F32)],
        compiler_params=_cparams(("arbitrary", "arbitrary"), 56),
        name="ffn_dense_ln",
    )(h, w_gate, w_up, w_down, x2d, ep, ln_g.reshape(1, d), ln_b.reshape(1, d))


def _router_kernel(h_ref, w_ref, b_ref, o_ref, hp_ref):
    logits = jnp.dot(h_ref[...], w_ref[...], preferred_element_type=F32) + b_ref[...]
    lane_i = lax.broadcasted_iota(jnp.int32, logits.shape, 1)
    lane = lane_i.astype(F32)
    lg = jnp.where(lane_i < N_EXPERTS, logits, -jnp.inf)
    m1 = jnp.max(lg, axis=1, keepdims=True)
    i1 = jnp.min(jnp.where(lg == m1, lane, float(LANES)), axis=1, keepdims=True)
    lg2 = jnp.where(lane == i1, -jnp.inf, lg)
    m2 = jnp.max(lg2, axis=1, keepdims=True)
    i2 = jnp.min(jnp.where(lg2 == m2, lane, float(LANES)), axis=1, keepdims=True)
    e2 = jnp.exp(m2 - m1)
    denom = 1.0 + e2
    out = jnp.where(lane_i == 0, i1, 0.0)
    out = jnp.where(lane_i == 1, i2, out)
    out = jnp.where(lane_i == 2, 1.0 / denom, out)
    out = jnp.where(lane_i == 3, e2 / denom, out)
    o_ref[...] = out
    _store_token_tiles(hp_ref, _pack_bf16_pairs(h_ref[...]))


def _pack_bf16_pairs(h):
    half = h.shape[1] // 2
    bits = pltpu.bitcast(h.astype(F32), jnp.uint32)
    return bits[:, :half] | lax.shift_right_logical(bits[:, half:], jnp.uint32(16))


def _store_token_tiles(ref, rows):
    m = rows.shape[0]
    n = rows.shape[1] // LANES
    for c in range(n):
        ref[pl.ds(c, m, stride=n), :] = rows[:, c * LANES:(c + 1) * LANES]


def _load_token_tiles(ref, n):
    m = ref.shape[0] // n
    return jnp.concatenate([ref[pl.ds(c, m, stride=n), :] for c in range(n)], axis=1)


def _unpack_bf16_pairs(packed):
    hi = pltpu.bitcast(packed & jnp.uint32(0xFFFF0000), F32)
    lo = pltpu.bitcast(lax.shift_left(packed, jnp.uint32(16)), F32)
    return jnp.concatenate([hi, lo], axis=1).astype(BF16)


def _router(h, w_router_pad, b_router_pad, tm=1024):
    t, d = h.shape
    return pl.pallas_call(
        _router_kernel,
        grid=(t // tm,),
        in_specs=[
            pl.BlockSpec((tm, d), lambda i: (i, 0)),
            pl.BlockSpec((d, LANES), lambda i: (0, 0)),
            pl.BlockSpec((1, LANES), lambda i: (0, 0)),
        ],
        out_specs=[pl.BlockSpec((tm, LANES), lambda i: (i, 0)),
                   pl.BlockSpec((tm * (d // 2 // LANES), LANES), lambda i: (i, 0))],
        out_shape=[jax.ShapeDtypeStruct((t, LANES), F32),
                   jax.ShapeDtypeStruct((t * (d // 2 // LANES), LANES), jnp.uint32)],
        compiler_params=_cparams(("arbitrary",), 32),
        name="moe_router",
    )(h, w_router_pad, b_router_pad)


def _gmm_kernel(step_rows, te_ref, tv_ref, src_ref, dst_ref, hp_ref, wg_ref, wu_ref, wd_ref, y_ref,
                xbuf_ref, x_ref, acc_ref, ybuf_ref, gather_sem, scatter_sem):
    i = pl.program_id(0)
    f = pl.program_id(1)
    n_tiles = pl.num_programs(0)
    n_f = pl.num_programs(1)
    tm = acc_ref.shape[0]
    n_in = xbuf_ref.shape[1] // tm
    n_out = ybuf_ref.shape[0] // tm
    valid = tv_ref[i] > 0
    slot = lax.rem(i, 2)
    prev_valid = jnp.logical_and(i > 0, tv_ref[jnp.maximum(i - 1, 0)] > 0)
    nxt = jnp.minimum(i + 1, n_tiles - 1)
    next_valid = jnp.logical_and(i + 1 < n_tiles, tv_ref[nxt] > 0)

    lo = jnp.minimum(f * step_rows, tm)
    hi = jnp.minimum(lo + step_rows, tm)

    def tile_rows(row, n):
        return pl.ds(pl.multiple_of(row * n, n), n)

    def for_each_row(first, stop, body):
        def group(g, carry):
            for u in range(SUBLANES):
                body(g * SUBLANES + u)
            return carry

        lax.fori_loop(first // SUBLANES, stop // SUBLANES, group, 0)

    def gather_rows(tile, to_slot, first, stop):
        base = tile * tm

        def row(r):
            pltpu.make_async_copy(hp_ref.at[tile_rows(src_ref[base + r], n_in), :],
                                  xbuf_ref.at[to_slot, tile_rows(r, n_in), :],
                                  gather_sem.at[to_slot]).start()

        for_each_row(first, stop, row)

    def gather_wait(from_slot):
        pltpu.make_async_copy(hp_ref.at[pl.ds(0, tm * n_in), :], xbuf_ref.at[from_slot],
                              gather_sem.at[from_slot]).wait()

    def scatter_rows(tile, first, stop):
        base = tile * tm

        def row(r):
            pltpu.make_async_copy(ybuf_ref.at[tile_rows(r, n_out), :],
                                  y_ref.at[tile_rows(dst_ref[base + r], n_out), :],
                                  scatter_sem.at[0]).start()

        for_each_row(first, stop, row)

    def scatter_wait():
        pltpu.make_async_copy(ybuf_ref, y_ref.at[pl.ds(0, tm * n_out), :],
                              scatter_sem.at[0]).wait()

    @pl.when(jnp.logical_and(f == 0, i == 0))
    def _():
        gather_rows(0, 0, 0, tm)

    @pl.when(valid)
    def _():
        @pl.when(f == 0)
        def _():
            gather_wait(slot)
            x_ref[...] = _unpack_bf16_pairs(_load_token_tiles(xbuf_ref.at[slot], n_in))

        @pl.when(next_valid)
        def _():
            gather_rows(nxt, 1 - slot, lo, hi)

        @pl.when(prev_valid)
        def _():
            scatter_rows(i - 1, lo, hi)

        def partial_out():
            return _swiglu_step(x_ref[...], wg_ref[...].astype(BF16), wu_ref[...].astype(BF16),
                                wd_ref[...].astype(BF16))

        @pl.when(f == 0)
        def _():
            acc_ref[...] = partial_out()

        @pl.when(f > 0)
        def _():
            acc_ref[...] += partial_out()

        @pl.when(f == n_f - 1)
        def _():
            @pl.when(prev_valid)
            def _():
                scatter_wait()

            _store_token_tiles(ybuf_ref, acc_ref[...])

            @pl.when(jnp.logical_not(next_valid))
            def _():
                scatter_rows(i, 0, tm)
                scatter_wait()


def _gmm(tile_expert, tile_valid, src_token, dst_row, hp, w_gate, w_up, w_down, layer, n_out_rows,
         tm, tf=256):
    r = src_token.shape[0]
    d = w_gate.shape[2]
    ff = w_gate.shape[3]
    nf = ff // tf

    def f_idx(i, f, tv):
        return jnp.where(tv[i] > 0, f, nf - 1)

    grid_spec = pltpu.PrefetchScalarGridSpec(
        num_scalar_prefetch=4,
        grid=(r // tm, nf),
        in_specs=[
            pl.BlockSpec(memory_space=pl.ANY),
            pl.BlockSpec((None, None, d, tf),
                         lambda i, f, te, tv, src, dst: (layer, te[i], 0, f_idx(i, f, tv))),
            pl.BlockSpec((None, None, d, tf),
                         lambda i, f, te, tv, src, dst: (layer, te[i], 0, f_idx(i, f, tv))),
            pl.BlockSpec((None, None, tf, d),
                         lambda i, f, te, tv, src, dst: (layer, te[i], f_idx(i, f, tv), 0)),
        ],
        out_specs=pl.BlockSpec(memory_space=pl.ANY),
        scratch_shapes=[
            pltpu.VMEM((2, tm * (d // 2 // LANES), LANES), jnp.uint32),
            pltpu.VMEM((tm, d), BF16),
            pltpu.VMEM((tm, d), F32),
            pltpu.VMEM((tm * (d // LANES), LANES), F32),
            pltpu.SemaphoreType.DMA((2,)),
            pltpu.SemaphoreType.DMA((1,)),
        ],
    )
    step_rows = pl.cdiv(pl.cdiv(tm, nf), SUBLANES) * SUBLANES
    return pl.pallas_call(
        functools.partial(_gmm_kernel, step_rows),
        grid_spec=grid_spec,
        out_shape=jax.ShapeDtypeStruct((n_out_rows * (d // LANES), LANES), F32),
        compiler_params=pltpu.CompilerParams(
            dimension_semantics=("arbitrary", "arbitrary"), vmem_limit_bytes=56 * MIB,
            disable_bounds_checks=True),
        name="moe_grouped_swiglu",
    )(tile_expert, tile_valid, src_token, dst_row, hp, w_gate, w_up, w_down)


def _combine_ln_kernel(alpha, route_ref, y1_ref, y2_ref, x_ref, ep_ref, lng_ref, lnb_ref,
                       xo_ref, ho_ref):
    ep = ep_ref[0]
    route = route_ref[...]
    n = x_ref.shape[1] // LANES
    y = (route[:, 2:3] * _load_token_tiles(y1_ref, n)
         + route[:, 3:4] * _load_token_tiles(y2_ref, n))
    x_new = _deepnorm_ln(x_ref[...], y, ep, lng_ref[...], lnb_ref[...], alpha)
    xo_ref[...] = x_new
    ho_ref[...] = _next_h(x_new, ep).astype(ho_ref.dtype)


def _combine_ln(route, y_slots, x2d, ep, ln_g, ln_b, alpha, seq, tm=256):
    t, d = x2d.shape
    per_b = seq // tm
    n_tiles = t // tm
    return pl.pallas_call(
        functools.partial(_combine_ln_kernel, alpha),
        grid=(n_tiles,),
        in_specs=[
            pl.BlockSpec((tm, LANES), lambda i: (i, 0)),
            pl.BlockSpec((tm * (d // LANES), LANES), lambda i: (i, 0)),
            pl.BlockSpec((tm * (d // LANES), LANES), lambda i: (i + n_tiles, 0)),
            pl.BlockSpec((tm, d), lambda i: (i, 0)),
            pl.BlockSpec((1, 3, d), lambda i: (i // per_b, 0, 0)),
            pl.BlockSpec((1, d), lambda i: (0, 0)),
            pl.BlockSpec((1, d), lambda i: (0, 0)),
        ],
        out_specs=[pl.BlockSpec((tm, d), lambda i: (i, 0)), pl.BlockSpec((tm, d), lambda i: (i, 0))],
        out_shape=[jax.ShapeDtypeStruct((t, d), F32), jax.ShapeDtypeStruct((t, d), BF16)],
        compiler_params=_cparams(("arbitrary",), 48),
        name="moe_combine_ln",
    )(route, y_slots, y_slots, x2d, ep, ln_g.reshape(1, d), ln_b.reshape(1, d))


def _dispatch_plan(route, tm):
    t = route.shape[0]
    n_pairs = 2 * t
    n_rows = n_pairs + N_EXPERTS * tm
    expert = route[:, :2].astype(jnp.int32).reshape(n_pairs)
    onehot = (expert[:, None] == jnp.arange(N_EXPERTS, dtype=jnp.int32)[None, :]).astype(jnp.int32)
    rank = jnp.take_along_axis(jnp.cumsum(onehot, axis=0) - onehot, expert[:, None], axis=1)[:, 0]
    counts = jnp.sum(onehot, axis=0)
    padded = ((counts + tm - 1) // tm) * tm
    ends = jnp.cumsum(padded)
    starts = ends - padded
    pos = starts[expert] + rank
    pair = jnp.full((n_rows,), -1, jnp.int32).at[pos].set(jnp.arange(n_pairs, dtype=jnp.int32))
    token = jnp.maximum(pair, 0) // 2
    is_pad = (pair < 0).astype(jnp.int32)
    src_token = jnp.where(pair >= 0, token, 0)
    dst_row = jnp.where(pair >= 0, (pair % 2) * t + token, n_pairs + jnp.cumsum(is_pad) - 1)
    tile_start = jnp.arange(n_rows // tm, dtype=jnp.int32) * tm
    n_ended = jnp.sum((tile_start[:, None] >= ends[None, :]).astype(jnp.int32), axis=1)
    tile_expert = jnp.minimum(n_ended, N_EXPERTS - 1)
    routed_end = (starts + counts)[tile_expert]
    tile_routed = jnp.where(tile_start < ends[-1], jnp.clip(routed_end - tile_start, 0, tm), 0)
    last_valid_expert = jnp.max(jnp.where(tile_routed > 0, tile_expert, 0))
    tile_expert = jnp.where(tile_routed > 0, tile_expert, last_valid_expert)
    return src_token, dst_row, tile_expert, tile_routed.astype(jnp.int32)


def kernel(x, c, w_in, conv_w, conv_b, w_rgate, b_rgate, w_igate, b_igate, lru_lambda, w_proj_a, w_proj_b, w_out, w_ada, b_ada, ln_mix_g, ln_mix_b, ln_ffn_g, ln_ffn_b, ffn_w_gate, ffn_w_up, ffn_w_down, moe_w_router, moe_b_router, moe_w_gate, moe_w_up, moe_w_down):
    batch, seq, d = x.shape
    depth = w_in.shape[0]
    t = batch * seq
    lru_w = conv_w.shape[2]
    attn_w = w_proj_b.shape[1]
    heads = attn_w // HEAD_DIM
    alpha = (2.0 * depth) ** 0.25
    moe_tm = 1024

    c_pad = jnp.zeros((SUBLANES, d), F32).at[:batch].set(c)
    ada = _ada_all_layers(c_pad, w_ada, b_ada)[:, :batch].reshape(depth, batch, 6, d)
    zeros_bd = jnp.zeros((batch, d), F32)

    def epilogue_params(gate, scale, shift):
        return jnp.stack([gate, scale, shift], axis=1)

    tables = _rope_tables(seq)
    x2d = x.reshape(t, d)
    h = _modulate(x2d, epilogue_params(zeros_bd, ada[0, :, 1], ada[0, :, 0]), seq)

    head_group_w = MOBA_HEADS_PER_STEP * HEAD_DIM
    q_col = 2 * lru_w // head_group_w
    k_col = q_col + attn_w // head_group_w
    v_col = k_col + attn_w // head_group_w
    ga_col = (2 * lru_w + 3 * attn_w) // (d // 2)
    gb_col = ga_col + 2

    for l in range(depth):
        sh1, sc1, g1, sh2, sc2, g2 = (ada[l, :, j] for j in range(6))
        proj = _matmul(h, w_in, l)
        ya = _lru(proj, conv_w[l], conv_b[l], w_rgate[l], b_rgate[l], w_igate[l], b_igate[l],
                  lru_lambda[l], batch, seq)
        attn = _moba(proj, tables, batch, seq, heads, q_col, k_col, v_col)
        x2d, h = _mix(ya, attn, proj, x2d, epilogue_params(g1, sc2, sh2),
                      w_proj_a[l].astype(BF16), w_proj_b[l].astype(BF16), w_out[l].astype(BF16),
                      ln_mix_g[l], ln_mix_b[l], alpha, seq, ga_col, gb_col)
        if l + 1 < depth:
            ep = epilogue_params(g2, ada[l + 1, :, 1], ada[l + 1, :, 0])
        else:
            ep = epilogue_params(g2, zeros_bd, zeros_bd)
        j = l // 2
        if l % 2 == 0:
            x2d, h = _ffn(h, x2d, ep, ffn_w_gate[j].astype(BF16), ffn_w_up[j].astype(BF16),
                          ffn_w_down[j].astype(BF16), ln_ffn_g[l], ln_ffn_b[l], alpha, seq)
        else:
            w_r_pad = jnp.zeros((d, LANES), BF16).at[:, :N_EXPERTS].set(moe_w_router[j].astype(BF16))
            b_r_pad = jnp.zeros((1, LANES), F32).at[0, :N_EXPERTS].set(moe_b_router[j])
            route, h_packed = _router(h, w_r_pad, b_r_pad)
            src_token, dst_row, tile_expert, tile_valid = _dispatch_plan(route, moe_tm)
            y_slots = _gmm(tile_expert, tile_valid, src_token, dst_row, h_packed, moe_w_gate,
                           moe_w_up, moe_w_down, j, src_token.shape[0], moe_tm)
            x2d, h = _combine_ln(route, y_slots, x2d, ep, ln_ffn_g[l], ln_ffn_b[l], alpha, seq)
    return x2d.reshape(batch, seq, d)
```

```python
import functools

import jax
import jax.numpy as jnp
from jax import lax
from jax.experimental import pallas as pl
from jax.experimental.pallas import tpu as pltpu

F32 = jnp.float32
BF16 = jnp.bfloat16

LRU_BLOCKS = 8
LRU_BLOCK_W = 128
CONV_WIDTH = 4
LRU_C = 8.0
HEAD_DIM = 128
MOBA_BLOCK = 256
MOBA_TOPK = 3
MOBA_HEADS_PER_STEP = 4
MOBA_SWEEP = 4
ROPE_THETA = 500000.0
ROT_DIM = HEAD_DIM // 4
ROT_HALF = ROT_DIM // 2
NEG_INF = -1e30
N_EXPERTS = 8
LN_EPS = 1e-5
LANES = 128
SUBLANES = 8
MIB = 1024 * 1024


def _cparams(semantics, vmem_mib):
    return pltpu.CompilerParams(dimension_semantics=semantics, vmem_limit_bytes=vmem_mib * MIB)


def _resident(block_shape, index_map):
    return pl.BlockSpec(block_shape, index_map, pipeline_mode=pl.Buffered(1))


def _ada_kernel(c_ref, w_ref, b_ref, o_ref):
    c = c_ref[...]
    c_act = c * jax.nn.sigmoid(c)
    o_ref[0] = jnp.dot(c_act, w_ref[0], preferred_element_type=F32) + b_ref[0]


def _ada_all_layers(c_pad, w_ada, b_ada, tn=1024):
    depth, d, n = w_ada.shape
    rows = c_pad.shape[0]
    return pl.pallas_call(
        _ada_kernel,
        grid=(depth, n // tn),
        in_specs=[
            pl.BlockSpec((rows, d), lambda l, j: (0, 0)),
            pl.BlockSpec((1, d, tn), lambda l, j: (l, 0, j)),
            pl.BlockSpec((1, 1, tn), lambda l, j: (l, 0, j)),
        ],
        out_specs=pl.BlockSpec((1, rows, tn), lambda l, j: (l, 0, j)),
        out_shape=jax.ShapeDtypeStruct((depth, rows, n), F32),
        compiler_params=_cparams(("arbitrary", "arbitrary"), 40),
        name="ada",
    )(c_pad, w_ada, b_ada.reshape(depth, 1, n))


def _modulate_kernel(x_ref, ep_ref, h_ref):
    ep = ep_ref[0]
    h_ref[...] = (x_ref[...] * (1.0 + ep[1:2]) + ep[2:3]).astype(h_ref.dtype)


def _modulate(x2d, ep, seq, tm=512):
    t, d = x2d.shape
    per_b = seq // tm
    return pl.pallas_call(
        _modulate_kernel,
        grid=(t // tm,),
        in_specs=[
            pl.BlockSpec((tm, d), lambda i: (i, 0)),
            pl.BlockSpec((1, 3, d), lambda i: (i // per_b, 0, 0)),
        ],
        out_specs=pl.BlockSpec((tm, d), lambda i: (i, 0)),
        out_shape=jax.ShapeDtypeStruct((t, d), BF16),
        compiler_params=_cparams(("arbitrary",), 32),
        name="modulate",
    )(x2d, ep)


def _matmul_kernel(a_ref, w_ref, o_ref, wb_ref):
    @pl.when(pl.program_id(1) == 0)
    def _():
        wb_ref[...] = w_ref[...].astype(wb_ref.dtype)

    o_ref[...] = jnp.dot(a_ref[...], wb_ref[...], preferred_element_type=F32).astype(o_ref.dtype)


def _matmul(a, w, layer, tm=1024, tn=1024):
    m, k = a.shape
    n = w.shape[2]
    return pl.pallas_call(
        _matmul_kernel,
        grid=(n // tn, m // tm),
        in_specs=[
            pl.BlockSpec((tm, k), lambda j, i: (i, 0)),
            pl.BlockSpec((None, k, tn), lambda j, i: (layer, 0, j)),
        ],
        out_specs=pl.BlockSpec((tm, tn), lambda j, i: (i, j)),
        out_shape=jax.ShapeDtypeStruct((m, n), BF16),
        scratch_shapes=[pltpu.VMEM((k, tn), BF16)],
        compiler_params=_cparams(("arbitrary", "arbitrary"), 48),
        name="in_proj",
    )(a, w)


def _gelu_tanh(x):
    return 0.5 * x * (1.0 + jnp.tanh(0.7978845608028654 * (x + 0.044715 * (x * x * x))))


def _lru_kernel(x_ref, g_ref, cw_ref, cb_ref, wr_ref, br_ref, wi_ref, bi_ref, lam_ref,
                o_ref, tail_ref, carry_ref):
    @pl.when(pl.program_id(1) == 0)
    def _():
        tail_ref[...] = jnp.zeros_like(tail_ref)
        carry_ref[...] = jnp.zeros_like(carry_ref)

    ts, w = x_ref.shape
    groups = ts // SUBLANES
    x3 = x_ref[...].astype(F32).reshape(groups, SUBLANES, w)
    xe = jnp.concatenate([tail_ref[...][None], x3], axis=0)
    sub = lax.broadcasted_iota(jnp.int32, (groups, SUBLANES, w), 1)

    y = cb_ref[...][None] + cw_ref[CONV_WIDTH - 1:CONV_WIDTH, :][None] * x3
    for k in range(1, CONV_WIDTH):
        rolled = pltpu.roll(xe, k, axis=1)
        shifted = jnp.where(sub < k, rolled[:-1], rolled[1:])
        y = y + cw_ref[CONV_WIDTH - 1 - k:CONV_WIDTH - k, :][None] * shifted
    tail_ref[...] = x3[groups - 1]

    xa = y.reshape(ts, w)
    xab = xa.astype(BF16)

    def block_diag(w_ref, b_ref):
        parts = [
            jnp.dot(xab[:, n * LRU_BLOCK_W:(n + 1) * LRU_BLOCK_W], w_ref[n],
                    preferred_element_type=F32)
            for n in range(LRU_BLOCKS)
        ]
        return jnp.concatenate(parts, axis=1) + b_ref[...]

    r = jax.nn.sigmoid(block_diag(wr_ref, br_ref))
    i = jax.nn.sigmoid(block_diag(wi_ref, bi_ref))
    neg_lam = -lam_ref[...]
    softplus = jnp.maximum(neg_lam, 0.0) + jnp.log1p(jnp.exp(-jnp.abs(neg_lam)))
    a = jnp.exp((-LRU_C) * r * softplus)
    u = jnp.sqrt(1.0 - a * a) * (i * xa)

    a3 = a.reshape(groups, SUBLANES, w)
    u3 = u.reshape(groups, SUBLANES, w)
    d = 1
    while d < SUBLANES:
        a_prev = pltpu.roll(a3, d, axis=1)
        u_prev = pltpu.roll(u3, d, axis=1)
        take = sub >= d
        u3 = jnp.where(take, a3 * u_prev + u3, u3)
        a3 = jnp.where(take, a3 * a_prev, a3)
        d *= 2
    carry = carry_ref[...]
    hs = []
    for gi in range(groups):
        hg = u3[gi] + a3[gi] * carry
        hs.append(hg)
        carry = hg[SUBLANES - 1:SUBLANES, :]
    carry_ref[...] = carry
    h = jnp.stack(hs, axis=0).reshape(ts, w)

    o_ref[...] = (h * _gelu_tanh(g_ref[...].astype(F32))).astype(o_ref.dtype)


def _lru(proj, conv_w, conv_b, w_r, b_r, w_i, b_i, lam, batch, seq, ts=256):
    t = proj.shape[0]
    w = conv_w.shape[1]
    per_b = seq // ts
    row = lambda b, s: b * per_b + s
    vec = lambda v: v.reshape(1, w)
    const2 = lambda b, s: (0, 0)
    const3 = lambda b, s: (0, 0, 0)
    return pl.pallas_call(
        _lru_kernel,
        grid=(batch, per_b),
        in_specs=[
            pl.BlockSpec((ts, w), lambda b, s: (row(b, s), 0)),
            pl.BlockSpec((ts, w), lambda b, s: (row(b, s), 1)),
            pl.BlockSpec((CONV_WIDTH, w), const2),
            pl.BlockSpec((1, w), const2),
            pl.BlockSpec((LRU_BLOCKS, LRU_BLOCK_W, LRU_BLOCK_W), const3),
            pl.BlockSpec((1, w), const2),
            pl.BlockSpec((LRU_BLOCKS, LRU_BLOCK_W, LRU_BLOCK_W), const3),
            pl.BlockSpec((1, w), const2),
            pl.BlockSpec((1, w), const2),
        ],
        out_specs=pl.BlockSpec((ts, w), lambda b, s: (row(b, s), 0)),
        out_shape=jax.ShapeDtypeStruct((t, w), BF16),
        scratch_shapes=[pltpu.VMEM((SUBLANES, w), F32), pltpu.VMEM((1, w), F32)],
        compiler_params=_cparams(("arbitrary", "arbitrary"), 48),
        name="conv_rglru",
    )(proj, proj, conv_w, vec(conv_b), w_r.astype(BF16), vec(b_r), w_i.astype(BF16), vec(b_i),
      vec(lam))


def _rotary(x, cos_ref, sin_lo_ref, sin_hi_ref, rows):
    up = pltpu.roll(x, HEAD_DIM - ROT_HALF, axis=1)
    down = pltpu.roll(x, ROT_HALF, axis=1)
    return x * cos_ref[rows, :] + up * sin_lo_ref[rows, :] + down * sin_hi_ref[rows, :]


_NT = (((1,), (1,)), ((), ()))


def _moba_kernel(q_ref, k_ref, v_ref, cos_ref, sin_lo_ref, sin_hi_ref, o_ref,
                 krot_ref, vt_ref, vtg_ref, kmean_ref, sel_ref, raw_ref):
    qi = pl.program_id(2)
    n_heads, n_blk = kmean_ref.shape[0], kmean_ref.shape[1]
    scale2 = HEAD_DIM ** -0.5 * 1.4426950408889634
    head_cols = [slice(hh * HEAD_DIM, (hh + 1) * HEAD_DIM) for hh in range(n_heads)]

    @pl.when(qi == 0)
    def _():
        for hh in range(n_heads):
            for n in range(n_blk):
                rows = pl.ds(n * MOBA_BLOCK, MOBA_BLOCK)
                kr = _rotary(k_ref[rows, head_cols[hh]].astype(F32), cos_ref, sin_lo_ref,
                             sin_hi_ref, rows)
                krot_ref[hh, rows, :] = kr.astype(krot_ref.dtype)
                kmean_ref[hh, n:n + 1, :] = jnp.mean(kr, axis=0, keepdims=True)
                vt = v_ref[rows, head_cols[hh]].astype(F32).T.astype(vt_ref.dtype)
                vt_ref[hh, n] = vt
                c = n % MOBA_SWEEP
                vtg_ref[hh, n // MOBA_SWEEP, :, c * MOBA_BLOCK:(c + 1) * MOBA_BLOCK] = vt

    own_rows = pl.ds(pl.multiple_of(qi * MOBA_BLOCK, MOBA_BLOCK), MOBA_BLOCK)

    qts, init = [], []
    for hh in range(n_heads):
        qt = _rotary(q_ref[:, head_cols[hh]].astype(F32), cos_ref, sin_lo_ref, sin_hi_ref,
                     own_rows).T.astype(BF16)
        qts.append(qt)

        gate = jnp.dot(kmean_ref[hh].astype(BF16), qt, preferred_element_type=F32)
        blk_i = lax.broadcasted_iota(jnp.int32, gate.shape, 0)
        blk = blk_i.astype(F32)
        past = blk_i < qi
        g = jnp.where(past, gate, -jnp.inf)
        sel = jnp.zeros_like(gate)
        for _ in range(MOBA_TOPK):
            best = jnp.max(g, axis=0, keepdims=True)
            idx = jnp.min(jnp.where(g == best, blk, float(n_blk)), axis=0, keepdims=True)
            pick = blk == idx
            sel = jnp.where(pick, 1.0, sel)
            g = jnp.where(pick, -jnp.inf, g)
        sel_ref[hh] = jnp.where(past, sel, 0.0)

        s = jnp.dot(krot_ref[hh, own_rows, :], qt, preferred_element_type=F32) * scale2
        key_id = lax.broadcasted_iota(jnp.int32, s.shape, 0)
        qry_id = lax.broadcasted_iota(jnp.int32, s.shape, 1)
        s = jnp.where(key_id <= qry_id, s, NEG_INF)
        m0 = jnp.max(s, axis=0, keepdims=True)
        p = jnp.exp2(s - m0)
        l0 = jnp.sum(p, axis=0, keepdims=True)
        acc0 = jnp.dot(vt_ref[hh, qi], p.astype(BF16), preferred_element_type=F32)
        init.append((m0, l0, acc0))

    span = MOBA_SWEEP * MOBA_BLOCK
    last_group = n_blk // MOBA_SWEEP - 1

    def raw_scores(hh, j):
        rows = pl.ds(pl.multiple_of(j * span, span), span)
        return jnp.dot(krot_ref[hh, rows, :], qts[hh], preferred_element_type=F32)

    for hh in range(n_heads):
        raw_ref[hh] = raw_scores(hh, 0)

    def past_group(j, carry):
        out = []
        for hh in range(n_heads):
            m, l, acc = carry[hh]
            s = jnp.concatenate([
                jnp.where(sel_ref[hh, pl.ds(j * MOBA_SWEEP + c, 1), :] > 0.0,
                          raw_ref[hh, c * MOBA_BLOCK:(c + 1) * MOBA_BLOCK, :] * scale2, NEG_INF)
                for c in range(MOBA_SWEEP)], axis=0)
            raw_ref[hh] = raw_scores(hh, jnp.minimum(j + 1, last_group))
            m_new = jnp.maximum(m, jnp.max(s, axis=0, keepdims=True))
            alpha = jnp.exp2(m - m_new)
            p = jnp.exp2(s - m_new)
            l = alpha * l + jnp.sum(p, axis=0, keepdims=True)
            acc = alpha * acc + jnp.dot(vtg_ref[hh, j], p.astype(BF16), preferred_element_type=F32)
            out.append((m_new, l, acc))
        return tuple(out)

    n_groups = lax.div(qi + (MOBA_SWEEP - 1), MOBA_SWEEP)
    final = lax.fori_loop(0, n_groups, past_group, tuple(init))
    for hh in range(n_heads):
        _, l, acc = final[hh]
        o_ref[:, head_cols[hh]] = (acc / l).T.astype(o_ref.dtype)


def _rope_tables(seq):
    pos = jnp.arange(seq, dtype=F32)
    inv_freq = ROPE_THETA ** (-jnp.arange(0, ROT_DIM, 2, dtype=F32) / ROT_DIM)
    ang = pos[:, None] * inv_freq[None, :]
    cos, sin = jnp.cos(ang), jnp.sin(ang)
    rest = HEAD_DIM - ROT_DIM
    cos_t = jnp.concatenate([cos, cos, jnp.ones((seq, rest), F32)], axis=1)
    sin_lo = jnp.concatenate([-sin, jnp.zeros((seq, HEAD_DIM - ROT_HALF), F32)], axis=1)
    sin_hi = jnp.concatenate([jnp.zeros((seq, ROT_HALF), F32), sin, jnp.zeros((seq, rest), F32)], axis=1)
    return cos_t, sin_lo, sin_hi


def _moba(proj, tables, batch, seq, heads, q_col, k_col, v_col, hg=MOBA_HEADS_PER_STEP):
    t = proj.shape[0]
    n_blk = seq // MOBA_BLOCK
    gw = hg * HEAD_DIM
    table_spec = _resident((seq, HEAD_DIM), lambda b, h, qi: (0, 0))
    return pl.pallas_call(
        _moba_kernel,
        grid=(batch, heads // hg, n_blk),
        in_specs=[
            pl.BlockSpec((MOBA_BLOCK, gw), lambda b, h, qi: (b * n_blk + qi, q_col + h)),
            pl.BlockSpec((seq, gw), lambda b, h, qi: (b, k_col + h)),
            pl.BlockSpec((seq, gw), lambda b, h, qi: (b, v_col + h)),
            table_spec, table_spec, table_spec,
        ],
        out_specs=pl.BlockSpec((MOBA_BLOCK, gw), lambda b, h, qi: (b * n_blk + qi, h)),
        out_shape=jax.ShapeDtypeStruct((t, heads * HEAD_DIM), BF16),
        scratch_shapes=[
            pltpu.VMEM((hg, seq, HEAD_DIM), BF16),
            pltpu.VMEM((hg, n_blk, HEAD_DIM, MOBA_BLOCK), BF16),
            pltpu.VMEM((hg, n_blk // MOBA_SWEEP, HEAD_DIM, MOBA_SWEEP * MOBA_BLOCK), BF16),
            pltpu.VMEM((hg, n_blk, HEAD_DIM), F32),
            pltpu.VMEM((hg, n_blk, MOBA_BLOCK), F32),
            pltpu.VMEM((hg, MOBA_SWEEP * MOBA_BLOCK, MOBA_BLOCK), F32),
        ],
        compiler_params=_cparams(("arbitrary", "arbitrary", "arbitrary"), 48),
        name="moba_attention",
    )(proj, proj, proj, *tables)


def _deepnorm_ln(x, y, ep, ln_g, ln_b, alpha):
    r = alpha * x + (1.0 + ep[0:1]) * y
    mu = jnp.mean(r, axis=-1, keepdims=True)
    rc = r - mu
    var = jnp.mean(rc * rc, axis=-1, keepdims=True)
    return rc * lax.rsqrt(var + LN_EPS) * ln_g + ln_b


def _next_h(x_new, ep):
    return x_new * (1.0 + ep[1:2]) + ep[2:3]


def _mix_kernel(alpha, ya_ref, at_ref, ga0_ref, ga1_ref, gb0_ref, gb1_ref, x_ref, ep_ref,
                wpa_ref, wpb_ref, wout_ref, lng_ref, lnb_ref, xo_ref, ho_ref):
    y_a = jnp.dot(ya_ref[...], wpa_ref[...], preferred_element_type=F32)
    y_b = jnp.dot(at_ref[...], wpb_ref[...], preferred_element_type=F32)
    gate_a = jnp.concatenate([ga0_ref[...], ga1_ref[...]], axis=1).astype(F32)
    gate_b = jnp.concatenate([gb0_ref[...], gb1_ref[...]], axis=1).astype(F32)
    merged = jax.nn.sigmoid(gate_a) * y_a + jax.nn.sigmoid(gate_b) * y_b
    y = jnp.dot(merged.astype(BF16), wout_ref[...], preferred_element_type=F32)
    ep = ep_ref[0]
    x_new = _deepnorm_ln(x_ref[...], y, ep, lng_ref[...], lnb_ref[...], alpha)
    xo_ref[...] = x_new
    ho_ref[...] = _next_h(x_new, ep).astype(ho_ref.dtype)


def _mix(ya, attn, proj, x2d, ep, w_pa, w_pb, w_out, ln_g, ln_b, alpha, seq, ga_col, gb_col, tm=256):
    t, d = x2d.shape
    wa = ya.shape[1]
    half = d // 2
    per_b = seq // tm
    const = lambda i: (0, 0)
    return pl.pallas_call(
        functools.partial(_mix_kernel, alpha),
        grid=(t // tm,),
        in_specs=[
            pl.BlockSpec((tm, wa), lambda i: (i, 0)),
            pl.BlockSpec((tm, wa), lambda i: (i, 0)),
            pl.BlockSpec((tm, half), lambda i: (i, ga_col)),
            pl.BlockSpec((tm, half), lambda i: (i, ga_col + 1)),
            pl.BlockSpec((tm, half), lambda i: (i, gb_col)),
            pl.BlockSpec((tm, half), lambda i: (i, gb_col + 1)),
            pl.BlockSpec((tm, d), lambda i: (i, 0)),
            pl.BlockSpec((1, 3, d), lambda i: (i // per_b, 0, 0)),
            _resident((wa, d), const),
            _resident((wa, d), const),
            _resident((d, d), const),
            _resident((1, d), const),
            _resident((1, d), const),
        ],
        out_specs=[pl.BlockSpec((tm, d), lambda i: (i, 0)), pl.BlockSpec((tm, d), lambda i: (i, 0))],
        out_shape=[jax.ShapeDtypeStruct((t, d), F32), jax.ShapeDtypeStruct((t, d), BF16)],
        compiler_params=_cparams(("arbitrary",), 56),
        name="mix_out_ln",
    )(ya, attn, proj, proj, proj, proj, x2d, ep, w_pa, w_pb, w_out, ln_g.reshape(1, d),
      ln_b.reshape(1, d))


def _swiglu_step(h, wg, wu, wd):
    g = jnp.dot(h, wg, preferred_element_type=F32)
    u = jnp.dot(h, wu, preferred_element_type=F32)
    return jnp.dot((g * jax.nn.sigmoid(g) * u).astype(BF16), wd, preferred_element_type=F32)


def _ffn_kernel(alpha, h_ref, wg_ref, wu_ref, wd_ref, x_ref, ep_ref, lng_ref, lnb_ref,
                xo_ref, ho_ref, acc_ref):
    f = pl.program_id(1)

    def partial_out():
        return _swiglu_step(h_ref[...], wg_ref[...], wu_ref[...], wd_ref[...])

    @pl.when(f == 0)
    def _():
        acc_ref[...] = partial_out()

    @pl.when(f > 0)
    def _():
        acc_ref[...] += partial_out()

    @pl.when(f == pl.num_programs(1) - 1)
    def _():
        ep = ep_ref[0]
        x_new = _deepnorm_ln(x_ref[...], acc_ref[...], ep, lng_ref[...], lnb_ref[...], alpha)
        xo_ref[...] = x_new
        ho_ref[...] = _next_h(x_new, ep).astype(ho_ref.dtype)


def _ffn(h, x2d, ep, w_gate, w_up, w_down, ln_g, ln_b, alpha, seq, tm=512, tf=512):
    t, d = x2d.shape
    ff = w_gate.shape[1]
    per_b = seq // tm
    return pl.pallas_call(
        functools.partial(_ffn_kernel, alpha),
        grid=(t // tm, ff // tf),
        in_specs=[
            pl.BlockSpec((tm, d), lambda i, f: (i, 0)),
            pl.BlockSpec((d, tf), lambda i, f: (0, f)),
            pl.BlockSpec((d, tf), lambda i, f: (0, f)),
            pl.BlockSpec((tf, d), lambda i, f: (f, 0)),
            pl.BlockSpec((tm, d), lambda i, f: (i, 0)),
            pl.BlockSpec((1, 3, d), lambda i, f: (i // per_b, 0, 0)),
            pl.BlockSpec((1, d), lambda i, f: (0, 0)),
            pl.BlockSpec((1, d), lambda i, f: (0, 0)),
        ],
        out_specs=[pl.BlockSpec((tm, d), lambda i, f: (i, 0)), pl.BlockSpec((tm, d), lambda i, f: (i, 0))],
        out_shape=[jax.ShapeDtypeStruct((t, d), F32), jax.ShapeDtypeStruct((t, d), BF16)],
        scratch_shapes=[pltpu.VMEM((tm, d), F32)],
        compiler_params=_cparams(("arbitrary", "arbitrary"), 56),
        name="ffn_dense_ln",
    )(h, w_gate, w_up, w_down, x2d, ep, ln_g.reshape(1, d), ln_b.reshape(1, d))


def _router_kernel(h_ref, w_ref, b_ref, o_ref, hp_ref):
    logits = jnp.dot(h_ref[...], w_ref[...], preferred_element_type=F32) + b_ref[...]
    lane_i = lax.broadcasted_iota(jnp.int32, logits.shape, 1)
    lane = lane_i.astype(F32)
    lg = jnp.where(lane_i < N_EXPERTS, logits, -jnp.inf)
    m1 = jnp.max(lg, axis=1, keepdims=True)
    i1 = jnp.min(jnp.where(lg == m1, lane, float(LANES)), axis=1, keepdims=True)
    lg2 = jnp.where(lane == i1, -jnp.inf, lg)
    m2 = jnp.max(lg2, axis=1, keepdims=True)
    i2 = jnp.min(jnp.where(lg2 == m2, lane, float(LANES)), axis=1, keepdims=True)
    e2 = jnp.exp(m2 - m1)
    denom = 1.0 + e2
    out = jnp.where(lane_i == 0, i1, 0.0)
    out = jnp.where(lane_i == 1, i2, out)
    out = jnp.where(lane_i == 2, 1.0 / denom, out)
    out = jnp.where(lane_i == 3, e2 / denom, out)
    o_ref[...] = out
    _store_token_tiles(hp_ref, _pack_bf16_pairs(h_ref[...]))


def _pack_bf16_pairs(h):
    half = h.shape[1] // 2
    bits = pltpu.bitcast(h.astype(F32), jnp.uint32)
    return bits[:, :half] | lax.shift_right_logical(bits[:, half:], jnp.uint32(16))


def _store_token_tiles(ref, rows):
    m = rows.shape[0]
    n = rows.shape[1] // LANES
    for c in range(n):
        ref[pl.ds(c, m, stride=n), :] = rows[:, c * LANES:(c + 1) * LANES]


def _load_token_tiles(ref, n):
    m = ref.shape[0] // n
    return jnp.concatenate([ref[pl.ds(c, m, stride=n), :] for c in range(n)], axis=1)


def _unpack_bf16_pairs(packed):
    hi = pltpu.bitcast(packed & jnp.uint32(0xFFFF0000), F32)
    lo = pltpu.bitcast(lax.shift_left(packed, jnp.uint32(16)), F32)
    return jnp.concatenate([hi, lo], axis=1).astype(BF16)


def _router(h, w_router_pad, b_router_pad, tm=1024):
    t, d = h.shape
    return pl.pallas_call(
        _router_kernel,
        grid=(t // tm,),
        in_specs=[
            pl.BlockSpec((tm, d), lambda i: (i, 0)),
            pl.BlockSpec((d, LANES), lambda i: (0, 0)),
            pl.BlockSpec((1, LANES), lambda i: (0, 0)),
        ],
        out_specs=[pl.BlockSpec((tm, LANES), lambda i: (i, 0)),
                   pl.BlockSpec((tm * (d // 2 // LANES), LANES), lambda i: (i, 0))],
        out_shape=[jax.ShapeDtypeStruct((t, LANES), F32),
                   jax.ShapeDtypeStruct((t * (d // 2 // LANES), LANES), jnp.uint32)],
        compiler_params=_cparams(("arbitrary",), 32),
        name="moe_router",
    )(h, w_router_pad, b_router_pad)


def _gmm_kernel(step_rows, te_ref, tv_ref, src_ref, dst_ref, hp_ref, wg_ref, wu_ref, wd_ref, y_ref,
                xbuf_ref, x_ref, acc_ref, ybuf_ref, gather_sem, scatter_sem):
    i = pl.program_id(0)
    f = pl.program_id(1)
    n_tiles = pl.num_programs(0)
    n_f = pl.num_programs(1)
    tm = acc_ref.shape[0]
    n_in = xbuf_ref.shape[1] // tm
    n_out = ybuf_ref.shape[0] // tm
    valid = tv_ref[i] > 0
    slot = lax.rem(i, 2)
    prev_valid = jnp.logical_and(i > 0, tv_ref[jnp.maximum(i - 1, 0)] > 0)
    nxt = jnp.minimum(i + 1, n_tiles - 1)
    next_valid = jnp.logical_and(i + 1 < n_tiles, tv_ref[nxt] > 0)

    lo = jnp.minimum(f * step_rows, tm)
    hi = jnp.minimum(lo + step_rows, tm)

    def tile_rows(row, n):
        return pl.ds(pl.multiple_of(row * n, n), n)

    def for_each_row(first, stop, body):
        def group(g, carry):
            for u in range(SUBLANES):
                body(g * SUBLANES + u, u)
            return carry

        lax.fori_loop(first // SUBLANES, stop // SUBLANES, group, 0)

    def gather_rows(tile, to_slot, first, stop):
        base = tile * tm

        def row(r, u):
            pltpu.make_async_copy(hp_ref.at[tile_rows(src_ref[base + r], n_in), :],
                                  xbuf_ref.at[to_slot, tile_rows(r, n_in), :],
                                  gather_sem.at[to_slot]).start()

        for_each_row(first, stop, row)

    def gather_wait(from_slot):
        pltpu.make_async_copy(hp_ref.at[pl.ds(0, tm * n_in), :], xbuf_ref.at[from_slot],
                              gather_sem.at[from_slot]).wait()

    def scatter_rows(tile, first, stop):
        base = tile * tm

        def row(r, u):
            pltpu.make_async_copy(ybuf_ref.at[tile_rows(r, n_out), :],
                                  y_ref.at[tile_rows(dst_ref[base + r], n_out), :],
                                  scatter_sem.at[0]).start(priority=u % 2)

        for_each_row(first, stop, row)

    def scatter_wait():
        pltpu.make_async_copy(ybuf_ref, y_ref.at[pl.ds(0, tm * n_out), :],
                              scatter_sem.at[0]).wait()

    @pl.when(jnp.logical_and(f == 0, i == 0))
    def _():
        gather_rows(0, 0, 0, tm)

    @pl.when(valid)
    def _():
        @pl.when(f == 0)
        def _():
            gather_wait(slot)
            x_ref[...] = _unpack_bf16_pairs(_load_token_tiles(xbuf_ref.at[slot], n_in))

        @pl.when(next_valid)
        def _():
            gather_rows(nxt, 1 - slot, lo, hi)

        @pl.when(prev_valid)
        def _():
            scatter_rows(i - 1, lo, hi)

        def partial_out():
            return _swiglu_step(x_ref[...], wg_ref[...].astype(BF16), wu_ref[...].astype(BF16),
                                wd_ref[...].astype(BF16))

        @pl.when(f == 0)
        def _():
            acc_ref[...] = partial_out()

        @pl.when(f > 0)
        def _():
            acc_ref[...] += partial_out()

        @pl.when(f == n_f - 1)
        def _():
            @pl.when(prev_valid)
            def _():
                scatter_wait()

            _store_token_tiles(ybuf_ref, acc_ref[...])

            @pl.when(jnp.logical_not(next_valid))
            def _():
                scatter_rows(i, 0, tm)
                scatter_wait()


def _gmm(tile_expert, tile_valid, src_token, dst_row, hp, w_gate, w_up, w_down, layer, n_out_rows,
         tm, tf=256):
    r = src_token.shape[0]
    d = w_gate.shape[2]
    ff = w_gate.shape[3]
    nf = ff // tf

    def f_idx(i, f, tv):
        return jnp.where(tv[i] > 0, f, nf - 1)

    grid_spec = pltpu.PrefetchScalarGridSpec(
        num_scalar_prefetch=4,
        grid=(r // tm, nf),
        in_specs=[
            pl.BlockSpec(memory_space=pl.ANY),
            pl.BlockSpec((None, None, d, tf),
                         lambda i, f, te, tv, src, dst: (layer, te[i], 0, f_idx(i, f, tv))),
            pl.BlockSpec((None, None, d, tf),
                         lambda i, f, te, tv, src, dst: (layer, te[i], 0, f_idx(i, f, tv))),
            pl.BlockSpec((None, None, tf, d),
                         lambda i, f, te, tv, src, dst: (layer, te[i], f_idx(i, f, tv), 0)),
        ],
        out_specs=pl.BlockSpec(memory_space=pl.ANY),
        scratch_shapes=[
            pltpu.VMEM((2, tm * (d // 2 // LANES), LANES), jnp.uint32),
            pltpu.VMEM((tm, d), BF16),
            pltpu.VMEM((tm, d), F32),
            pltpu.VMEM((tm * (d // LANES), LANES), F32),
            pltpu.SemaphoreType.DMA((2,)),
            pltpu.SemaphoreType.DMA((1,)),
        ],
    )
    step_rows = pl.cdiv(pl.cdiv(tm, nf), SUBLANES) * SUBLANES
    return pl.pallas_call(
        functools.partial(_gmm_kernel, step_rows),
        grid_spec=grid_spec,
        out_shape=jax.ShapeDtypeStruct((n_out_rows * (d // LANES), LANES), F32),
        compiler_params=pltpu.CompilerParams(
            dimension_semantics=("arbitrary", "arbitrary"), vmem_limit_bytes=56 * MIB,
            disable_bounds_checks=True),
        name="moe_grouped_swiglu",
    )(tile_expert, tile_valid, src_token, dst_row, hp, w_gate, w_up, w_down)


def _combine_ln_kernel(alpha, route_ref, y1_ref, y2_ref, x_ref, ep_ref, lng_ref, lnb_ref,
                       xo_ref, ho_ref):
    ep = ep_ref[0]
    route = route_ref[...]
    n = x_ref.shape[1] // LANES
    y = (route[:, 2:3] * _load_token_tiles(y1_ref, n)
         + route[:, 3:4] * _load_token_tiles(y2_ref, n))
    x_new = _deepnorm_ln(x_ref[...], y, ep, lng_ref[...], lnb_ref[...], alpha)
    xo_ref[...] = x_new
    ho_ref[...] = _next_h(x_new, ep).astype(ho_ref.dtype)


def _combine_ln(route, y_slots, x2d, ep, ln_g, ln_b, alpha, seq, tm=256):
    t, d = x2d.shape
    per_b = seq // tm
    n_tiles = t // tm
    return pl.pallas_call(
        functools.partial(_combine_ln_kernel, alpha),
        grid=(n_tiles,),
        in_specs=[
            pl.BlockSpec((tm, LANES), lambda i: (i, 0)),
            pl.BlockSpec((tm * (d // LANES), LANES), lambda i: (i, 0)),
            pl.BlockSpec((tm * (d // LANES), LANES), lambda i: (i + n_tiles, 0)),
            pl.BlockSpec((tm, d), lambda i: (i, 0)),
            pl.BlockSpec((1, 3, d), lambda i: (i // per_b, 0, 0)),
            pl.BlockSpec((1, d), lambda i: (0, 0)),
            pl.BlockSpec((1, d), lambda i: (0, 0)),
        ],
        out_specs=[pl.BlockSpec((tm, d), lambda i: (i, 0)), pl.BlockSpec((tm, d), lambda i: (i, 0))],
        out_shape=[jax.ShapeDtypeStruct((t, d), F32), jax.ShapeDtypeStruct((t, d), BF16)],
        compiler_params=_cparams(("arbitrary",), 48),
        name="moe_combine_ln",
    )(route, y_slots, y_slots, x2d, ep, ln_g.reshape(1, d), ln_b.reshape(1, d))


def _dispatch_plan(route, tm):
    t = route.shape[0]
    n_pairs = 2 * t
    n_rows = n_pairs + N_EXPERTS * tm
    expert = route[:, :2].astype(jnp.int32).reshape(n_pairs)
    onehot = (expert[:, None] == jnp.arange(N_EXPERTS, dtype=jnp.int32)[None, :]).astype(jnp.int32)
    rank = jnp.take_along_axis(jnp.cumsum(onehot, axis=0) - onehot, expert[:, None], axis=1)[:, 0]
    counts = jnp.sum(onehot, axis=0)
    padded = ((counts + tm - 1) // tm) * tm
    ends = jnp.cumsum(padded)
    starts = ends - padded
    pos = starts[expert] + rank
    pair = jnp.full((n_rows,), -1, jnp.int32).at[pos].set(jnp.arange(n_pairs, dtype=jnp.int32))
    token = jnp.maximum(pair, 0) // 2
    is_pad = (pair < 0).astype(jnp.int32)
    src_token = jnp.where(pair >= 0, token, 0)
    dst_row = jnp.where(pair >= 0, (pair % 2) * t + token, n_pairs + jnp.cumsum(is_pad) - 1)
    tile_start = jnp.arange(n_rows // tm, dtype=jnp.int32) * tm
    n_ended = jnp.sum((tile_start[:, None] >= ends[None, :]).astype(jnp.int32), axis=1)
    tile_expert = jnp.minimum(n_ended, N_EXPERTS - 1)
    routed_end = (starts + counts)[tile_expert]
    tile_routed = jnp.where(tile_start < ends[-1], jnp.clip(routed_end - tile_start, 0, tm), 0)
    last_valid_expert = jnp.max(jnp.where(tile_routed > 0, tile_expert, 0))
    tile_expert = jnp.where(tile_routed > 0, tile_expert, last_valid_expert)
    return src_token, dst_row, tile_expert, tile_routed.astype(jnp.int32)


def kernel(x, c, w_in, conv_w, conv_b, w_rgate, b_rgate, w_igate, b_igate, lru_lambda, w_proj_a, w_proj_b, w_out, w_ada, b_ada, ln_mix_g, ln_mix_b, ln_ffn_g, ln_ffn_b, ffn_w_gate, ffn_w_up, ffn_w_down, moe_w_router, moe_b_router, moe_w_gate, moe_w_up, moe_w_down):
    batch, seq, d = x.shape
    depth = w_in.shape[0]
    t = batch * seq
    lru_w = conv_w.shape[2]
    attn_w = w_proj_b.shape[1]
    heads = attn_w // HEAD_DIM
    alpha = (2.0 * depth) ** 0.25
    moe_tm = 1024

    c_pad = jnp.zeros((SUBLANES, d), F32).at[:batch].set(c)
    ada = _ada_all_layers(c_pad, w_ada, b_ada)[:, :batch].reshape(depth, batch, 6, d)
    zeros_bd = jnp.zeros((batch, d), F32)

    def epilogue_params(gate, scale, shift):
        return jnp.stack([gate, scale, shift], axis=1)

    tables = _rope_tables(seq)
    x2d = x.reshape(t, d)
    h = _modulate(x2d, epilogue_params(zeros_bd, ada[0, :, 1], ada[0, :, 0]), seq)

    head_group_w = MOBA_HEADS_PER_STEP * HEAD_DIM
    q_col = 2 * lru_w // head_group_w
    k_col = q_col + attn_w // head_group_w
    v_col = k_col + attn_w // head_group_w
    ga_col = (2 * lru_w + 3 * attn_w) // (d // 2)
    gb_col = ga_col + 2

    for l in range(depth):
        sh1, sc1, g1, sh2, sc2, g2 = (ada[l, :, j] for j in range(6))
        proj = _matmul(h, w_in, l)
        ya = _lru(proj, conv_w[l], conv_b[l], w_rgate[l], b_rgate[l], w_igate[l], b_igate[l],
                  lru_lambda[l], batch, seq)
        attn = _moba(proj, tables, batch, seq, heads, q_col, k_col, v_col)
        x2d, h = _mix(ya, attn, proj, x2d, epilogue_params(g1, sc2, sh2),
                      w_proj_a[l].astype(BF16), w_proj_b[l].astype(BF16), w_out[l].astype(BF16),
                      ln_mix_g[l], ln_mix_b[l], alpha, seq, ga_col, gb_col)
        if l + 1 < depth:
            ep = epilogue_params(g2, ada[l + 1, :, 1], ada[l + 1, :, 0])
        else:
            ep = epilogue_params(g2, zeros_bd, zeros_bd)
        j = l // 2
        if l % 2 == 0:
            x2d, h = _ffn(h, x2d, ep, ffn_w_gate[j].astype(BF16), ffn_w_up[j].astype(BF16),
                          ffn_w_down[j].astype(BF16), ln_ffn_g[l], ln_ffn_b[l], alpha, seq)
        else:
            w_r_pad = jnp.zeros((d, LANES), BF16).at[:, :N_EXPERTS].set(moe_w_router[j].astype(BF16))
            b_r_pad = jnp.zeros((1, LANES), F32).at[0, :N_EXPERTS].set(moe_b_router[j])
            route, h_packed = _router(h, w_r_pad, b_r_pad)
            src_token, dst_row, tile_expert, tile_valid = _dispatch_plan(route, moe_tm)
            y_slots = _gmm(tile_expert, tile_valid, src_token, dst_row, h_packed, moe_w_gate,
                           moe_w_up, moe_w_down, j, src_token.shape[0], moe_tm)
            x2d, h = _combine_ln(route, y_slots, x2d, ep, ln_ffn_g[l], ln_ffn_b[l], alpha, seq)
    return x2d.reshape(batch, seq, d)
```
